```python
import math
import jax, jax.numpy as jnp
from jax import lax
import numpy as np

D_MODEL = 2048
BATCH = 16
SEQ = 2048
DEPTH = 2

HEAD_DIM = 128
N_HEADS = D_MODEL // HEAD_DIM
A_Q_HEADS = N_HEADS // 2
A_KV_HEADS = A_Q_HEADS // 4
A_HALF_WINDOW = 128
B_HEADS = N_HEADS - A_Q_HEADS
B_CONFIGS = ((128, 1), (512, 4), (2048, 16))
C_HEADS = N_HEADS
GRID_W = 64
NA_KH = 8
NA_KW = 16
RMS_EPS = 1e-5
NEG = -1e30

A_Q_COLS = A_Q_HEADS * HEAD_DIM
A_KV_COLS = A_KV_HEADS * HEAD_DIM
B_COLS = B_HEADS * HEAD_DIM
AB_WIDTH = A_Q_COLS + B_COLS
AB_SPLITS = (A_Q_COLS,
             A_Q_COLS + A_KV_COLS,
             A_Q_COLS + 2 * A_KV_COLS,
             A_Q_COLS + 2 * A_KV_COLS + B_COLS,
             A_Q_COLS + 2 * A_KV_COLS + 2 * B_COLS,
             A_Q_COLS + 2 * A_KV_COLS + 3 * B_COLS)
AB_IN_COLS = A_Q_COLS + 2 * A_KV_COLS + 3 * B_COLS + AB_WIDTH
C_WIDTH = C_HEADS * HEAD_DIM
C_IN_COLS = 4 * C_WIDTH

kernel_name = "hybrid_window_dilated_neighbourhood_encoder"


def _rmsnorm(x, g):
    xf = x.astype(jnp.float32)
    y = xf * lax.rsqrt(jnp.mean(xf * xf, axis=-1, keepdims=True) + RMS_EPS)
    return (y * g.astype(jnp.float32)).astype(x.dtype)


def _alibi_slopes(n):
    return jnp.exp2(-8.0 * jnp.arange(1, n + 1, dtype=jnp.float32) / n)


def _banded_attention(q, k, v, half, dist_scale, slopes, sink=None):
    n, L, hq, dh = q.shape
    hkv = k.shape[2]
    g = hq // hkv
    blk = half
    nb = -(-L // blk)
    lp = nb * blk
    qp = jnp.pad(q, ((0, 0), (0, lp - L), (0, 0), (0, 0)))
    kvpad = ((0, 0), (blk, lp - L + blk), (0, 0), (0, 0))
    kp = jnp.pad(k, kvpad).reshape(n, nb + 2, blk, hkv, dh)
    vp = jnp.pad(v, kvpad).reshape(n, nb + 2, blk, hkv, dh)
    kw = jnp.concatenate([kp[:, :-2], kp[:, 1:-1], kp[:, 2:]], axis=2)
    vw = jnp.concatenate([vp[:, :-2], vp[:, 1:-1], vp[:, 2:]], axis=2)
    qb = qp.reshape(n, nb, blk, hkv, g, dh)
    s = jnp.einsum('nbqkgd,nbskd->nbkgqs', qb, kw).astype(jnp.float32) * (dh ** -0.5)
    rel = jnp.arange(3 * blk)[None, :] - blk - jnp.arange(blk)[:, None]
    kpos = jnp.arange(nb)[:, None] * blk - blk + jnp.arange(3 * blk)[None, :]
    valid = (jnp.abs(rel) <= half)[None] & ((kpos >= 0) & (kpos < L))[:, None, :]
    dist = (jnp.abs(rel) * dist_scale).astype(jnp.float32)
    bias = -slopes.astype(jnp.float32).reshape(hkv, g)[:, :, None, None] * dist
    s = jnp.where(valid[None, :, None, None], s + bias, NEG)
    m = jnp.max(s, axis=-1)
    if sink is not None:
        sk = sink.astype(jnp.float32).reshape(hkv, g)[None, None, :, :, None]
        m = jnp.maximum(m, sk)
    e = jnp.exp(s - m[..., None])
    denom = jnp.sum(e, axis=-1)
    if sink is not None:
        denom = denom + jnp.exp(sk - m)
    p = e / denom[..., None]
    o = jnp.einsum('nbkgqs,nbskd->nbqkgd', p.astype(v.dtype), vw).reshape(n, lp, hq, dh)[:, :L]
    lse = (m + jnp.log(denom)).transpose(0, 1, 4, 2, 3).reshape(n, lp, hq)[:, :L]
    return o, lse


def _dilated_mixture(q, k, v, slopes):
    b, s, h, dh = q.shape
    outs, lses = [], []
    for window, d in B_CONFIGS:
        ls = s // d

        def fold(t):
            return t.reshape(b, ls, d, h, dh).transpose(0, 2, 1, 3, 4).reshape(b * d, ls, h, dh)

        o, lse = _banded_attention(fold(q), fold(k), fold(v), window // (2 * d), d, slopes)
        outs.append(o.reshape(b, d, ls, h, dh).transpose(0, 2, 1, 3, 4).reshape(b, s, h, dh))
        lses.append(lse.reshape(b, d, ls, h).transpose(0, 2, 1, 3).reshape(b, s, h))
    w = jax.nn.softmax(jnp.stack(lses), axis=0)
    out = jnp.einsum('cbsh,cbshd->bshd', w, jnp.stack(outs).astype(jnp.float32))
    return out.astype(q.dtype)


def _neighbourhood_attention(q, k, v, rpb):
    b, s, h, dh = q.shape
    rows = s // GRID_W
    kh = min(NA_KH, rows)
    kw = NA_KW
    qg = q.reshape(b, rows, GRID_W, h, dh).transpose(1, 0, 2, 3, 4)
    kg = k.reshape(b, rows, GRID_W, h, dh)
    vg = v.reshape(b, rows, GRID_W, h, dh)
    col = jnp.arange(GRID_W)
    col_start = jnp.clip(col - kw // 2, 0, GRID_W - kw)
    col_mask = (col[None, :] >= col_start[:, None]) & (col[None, :] < col_start[:, None] + kw)
    dc = jnp.clip(col[None, :] - col[:, None] + NA_KW - 1, 0, 2 * NA_KW - 2)
    rpb_cols = rpb[:, :, dc]
    scale = dh ** -0.5

    def one_row(args):
        r, q_r = args
        r0 = jnp.clip(r - kh // 2, 0, rows - kh)
        k_r = lax.dynamic_slice_in_dim(kg, r0, kh, axis=1)
        v_r = lax.dynamic_slice_in_dim(vg, r0, kh, axis=1)
        dr = r0 + jnp.arange(kh) - r + NA_KH - 1
        bias = rpb_cols[:, dr].transpose(0, 2, 1, 3).astype(jnp.float32)
        sc = jnp.einsum('bqhd,bkwhd->bhqkw', q_r, k_r).astype(jnp.float32) * scale + bias[None]
        sc = jnp.where(col_mask[None, None, :, None, :], sc, NEG)
        p = jax.nn.softmax(sc.reshape(b, h, GRID_W, kh * GRID_W), axis=-1).reshape(b, h, GRID_W, kh, GRID_W)
        return jnp.einsum('bhqkw,bkwhd->bqhd', p.astype(v.dtype), v_r)

    out = lax.map(one_row, (jnp.arange(rows), qg))
    return out.transpose(1, 0, 2, 3, 4).reshape(b, s, h, dh)


def _layer_ab(x, ln, w_in, sink, w_out):
    b, s, _ = x.shape
    hn = _rmsnorm(x, ln)
    proj = hn @ w_in
    aq, ak, av, bq, bk, bv, z = jnp.split(proj, AB_SPLITS, axis=-1)
    slopes = _alibi_slopes(A_Q_HEADS + B_HEADS)
    ya, _ = _banded_attention(aq.reshape(b, s, A_Q_HEADS, HEAD_DIM),
                              ak.reshape(b, s, A_KV_HEADS, HEAD_DIM),
                              av.reshape(b, s, A_KV_HEADS, HEAD_DIM),
                              A_HALF_WINDOW, 1, slopes[:A_Q_HEADS], sink)
    yb = _dilated_mixture(bq.reshape(b, s, B_HEADS, HEAD_DIM),
                          bk.reshape(b, s, B_HEADS, HEAD_DIM),
                          bv.reshape(b, s, B_HEADS, HEAD_DIM),
                          slopes[A_Q_HEADS:])
    y = jnp.concatenate([ya.reshape(b, s, A_Q_COLS), yb.reshape(b, s, B_COLS)], axis=-1)
    y = y * jax.nn.silu(z)
    return x + y @ w_out


def _layer_c(x, ln, w_in, rpb, w_out):
    b, s, _ = x.shape
    hn = _rmsnorm(x, ln)
    proj = hn @ w_in
    cq, ck, cv, z = jnp.split(proj, 4, axis=-1)
    y = _neighbourhood_attention(cq.reshape(b, s, C_HEADS, HEAD_DIM),
                                 ck.reshape(b, s, C_HEADS, HEAD_DIM),
                                 cv.reshape(b, s, C_HEADS, HEAD_DIM), rpb)
    y = y.reshape(b, s, C_WIDTH) * jax.nn.silu(z)
    return x + y @ w_out


def setup_inputs(seed: int = 0) -> dict:
    key = jax.random.key(seed)
    ks = jax.random.split(key, 11)
    n_even = (DEPTH + 1) // 2
    n_odd = DEPTH // 2
    d = D_MODEL
    f32 = jnp.float32
    x = jax.random.normal(ks[0], (BATCH, SEQ, d), f32)
    ln_ab = 1.0 + 0.02 * jax.random.normal(ks[1], (n_even, d), f32)
    w_in_ab = jax.random.normal(ks[2], (n_even, d, AB_IN_COLS), f32) * d ** -0.5
    sink_a = 0.5 * jax.random.normal(ks[3], (n_even, A_Q_HEADS), f32)
    w_out_ab = jax.random.normal(ks[4], (n_even, AB_WIDTH, d), f32) * AB_WIDTH ** -0.5
    ln_c = 1.0 + 0.02 * jax.random.normal(ks[5], (n_odd, d), f32)
    w_in_c = jax.random.normal(ks[6], (n_odd, d, C_IN_COLS), f32) * d ** -0.5
    rpb_c = 0.1 * jax.random.normal(ks[7], (n_odd, C_HEADS, 2 * NA_KH - 1, 2 * NA_KW - 1), f32)
    w_out_c = jax.random.normal(ks[8], (n_odd, C_WIDTH, d), f32) * C_WIDTH ** -0.5
    ln_f = 1.0 + 0.02 * jax.random.normal(ks[9], (d,), f32)
    return {"x": x, "ln_ab": ln_ab, "w_in_ab": w_in_ab, "sink_a": sink_a, "w_out_ab": w_out_ab,
            "ln_c": ln_c, "w_in_c": w_in_c, "rpb_c": rpb_c, "w_out_c": w_out_c, "ln_f": ln_f}


def reference(x, ln_ab, w_in_ab, sink_a, w_out_ab, ln_c, w_in_c, rpb_c, w_out_c, ln_f):
    for layer in range(DEPTH):
        i = layer // 2
        if layer % 2 == 0:
            x = _layer_ab(x, ln_ab[i], w_in_ab[i], sink_a[i], w_out_ab[i])
        else:
            x = _layer_c(x, ln_c[i], w_in_c[i], rpb_c[i], w_out_c[i])
    return _rmsnorm(x, ln_f)
```

```python
import functools

import numpy as np
import jax
import jax.numpy as jnp
from jax import lax
from jax.experimental import pallas as pl
from jax.experimental.pallas import tpu as pltpu

HEAD_DIM = 128
A_Q_HEADS = 8
A_KV_HEADS = 2
A_GROUP = A_Q_HEADS // A_KV_HEADS
A_HALF_WINDOW = 128
B_HEADS = 8
B_CONFIGS = ((128, 1), (512, 4), (2048, 16))
B_HALF = 64
C_HEADS = 16
GRID_W = 64
NA_KH = 8
NA_KW = 16
RMS_EPS = 1e-5
NEG = -1e30
SCALE = HEAD_DIM ** -0.5

A_Q_COL = 0
A_K_COL = A_Q_HEADS
A_V_COL = A_K_COL + A_KV_HEADS
B_Q_COL = A_V_COL + A_KV_HEADS
B_K_COL = B_Q_COL + B_HEADS
B_V_COL = B_K_COL + B_HEADS
Z_AB_COL = B_V_COL + B_HEADS

NA_QROWS = 4
NA_KROWS = 12

VMEM_LIMIT = 56 * 1024 * 1024

_NT = (((1,), (1,)), ((), ()))


def _cparams(sem):
    return pltpu.CompilerParams(dimension_semantics=sem, vmem_limit_bytes=VMEM_LIMIT)


def _norm_proj_kernel(x_ref, g_ref, w_ref, o_ref, hn_ref):
    @pl.when(pl.program_id(1) == 0)
    def _():
        x = x_ref[...]
        ms = jnp.mean(x * x, axis=-1, keepdims=True)
        hn_ref[...] = (x * lax.rsqrt(ms + RMS_EPS) * g_ref[...]).astype(hn_ref.dtype)

    o_ref[...] = jnp.dot(hn_ref[...], w_ref[...], preferred_element_type=jnp.float32).astype(o_ref.dtype)


def _norm_proj(x2d, g, w, *, tm, tn, name):
    m, d = x2d.shape
    n = w.shape[1]
    return pl.pallas_call(
        _norm_proj_kernel,
        grid=(m // tm, n // tn),
        in_specs=[
            pl.BlockSpec((tm, d), lambda i, j: (i, 0)),
            pl.BlockSpec((1, d), lambda i, j: (0, 0)),
            pl.BlockSpec((d, tn), lambda i, j: (0, j)),
        ],
        out_specs=pl.BlockSpec((tm, tn), lambda i, j: (i, j)),
        out_shape=jax.ShapeDtypeStruct((m, n), jnp.bfloat16),
        scratch_shapes=[pltpu.VMEM((tm, d), jnp.bfloat16)],
        compiler_params=_cparams(("parallel", "arbitrary")),
        name=name,
    )(x2d, g.reshape(1, d), w)


def _alibi_slopes(n):
    return jnp.exp2(-8.0 * jnp.arange(1, n + 1, dtype=jnp.float32) / n)


def _band_bias(nq, nk, off, half, dist_scale, slopes):
    rel = np.arange(nk)[None, :] - np.arange(nq)[:, None] - off
    valid = np.abs(rel) <= half
    dist = jnp.asarray((np.abs(rel) * dist_scale).astype(np.float32))
    bias = -slopes[:, None, None] * dist
    return jnp.where(valid[None], bias, NEG)


def _na_bias_tables(rpb):
    h = rpb.shape[0]
    rows = 2048 // GRID_W
    cq = np.arange(GRID_W)
    c0 = np.clip(cq - NA_KW // 2, 0, GRID_W - NA_KW)
    ck = np.arange(GRID_W)
    col_valid = (ck[None, :] >= c0[:, None]) & (ck[None, :] < c0[:, None] + NA_KW)
    pad = GRID_W - NA_KW
    rp = jnp.pad(rpb, ((0, 0), (0, 0), (pad, pad)))
    a = jnp.stack([rp[:, :, GRID_W - 1 - q:2 * GRID_W - 1 - q] for q in range(GRID_W)], axis=2)
    a = jnp.where(col_valid[None, None], a, NEG)
    n_groups = rows // NA_QROWS
    tabs = []
    for g, ws in ((0, 0), (1, 0), (n_groups - 1, rows - NA_KROWS)):
        blocks = []
        for rr in range(NA_QROWS):
            r = g * NA_QROWS + rr
            r0 = int(np.clip(r - NA_KH // 2, 0, rows - NA_KH))
            per_wr = []
            for wr in range(NA_KROWS):
                kr = ws + wr
                if r0 <= kr < r0 + NA_KH:
                    per_wr.append(a[:, kr - r + NA_KH - 1])
                else:
                    per_wr.append(jnp.full((h, GRID_W, GRID_W), NEG, jnp.float32))
            blocks.append(jnp.stack(per_wr, axis=2))
        tabs.append(jnp.stack(blocks, axis=1).reshape(h, NA_QROWS * GRID_W, NA_KROWS * GRID_W))
    return jnp.stack(tabs, axis=1)


def _attn_a_kernel(sink_ref, q_ref, k_ref, v_ref, tbl_ref, o_ref, *, seq):
    blk = A_HALF_WINDOW
    nblk = seq // blk
    g = pl.program_id(1)
    sinks = [sink_ref[g * A_GROUP + j] for j in range(A_GROUP)]

    def block(qrow, krow, nk, c0):
        q4 = q_ref[0, pl.ds(qrow, blk), :]
        qs = jnp.concatenate([q4[:, j * HEAD_DIM:(j + 1) * HEAD_DIM] for j in range(A_GROUP)], axis=0)
        kb = k_ref[0, pl.ds(krow, nk), :]
        vb = v_ref[0, pl.ds(krow, nk), :]
        s = lax.dot_general(qs, kb, _NT, preferred_element_type=jnp.float32)
        es, ls = [], []
        for j in range(A_GROUP):
            t = s[j * blk:(j + 1) * blk, :] * SCALE + tbl_ref[j, :, c0:c0 + nk]
            m = jnp.maximum(jnp.max(t, axis=-1, keepdims=True), sinks[j])
            e = jnp.exp(t - m)
            ls.append(jnp.sum(e, axis=-1, keepdims=True) + jnp.exp(sinks[j] - m))
            es.append(e.astype(jnp.bfloat16))
        o = jnp.dot(jnp.concatenate(es, axis=0), vb, preferred_element_type=jnp.float32)
        out = jnp.concatenate([o[j * blk:(j + 1) * blk] / ls[j] for j in range(A_GROUP)], axis=1)
        o_ref[0, pl.ds(qrow, blk), :] = out.astype(o_ref.dtype)

    block(0, 0, 2 * blk, blk)

    def body(i, carry):
        qrow = pl.multiple_of(i * blk, blk)
        krow = pl.multiple_of((i - 1) * blk, blk)
        block(qrow, krow, 3 * blk, 0)
        return carry

    lax.fori_loop(1, nblk - 1, body, 0)
    block((nblk - 1) * blk, (nblk - 2) * blk, 2 * blk, 0)


def _attn_a(p3, sink, tbl):
    b, s, _ = p3.shape
    gw = A_GROUP * HEAD_DIM
    return pl.pallas_call(
        functools.partial(_attn_a_kernel, seq=s),
        grid=(b, A_KV_HEADS),
        in_specs=[
            pl.BlockSpec(memory_space=pltpu.SMEM),
            pl.BlockSpec((1, s, gw), lambda bi, g: (bi, 0, g)),
            pl.BlockSpec((1, s, HEAD_DIM), lambda bi, g: (bi, 0, A_K_COL + g)),
            pl.BlockSpec((1, s, HEAD_DIM), lambda bi, g: (bi, 0, A_V_COL + g)),
            pl.BlockSpec((A_GROUP, A_HALF_WINDOW, 3 * A_HALF_WINDOW), lambda bi, g: (g, 0, 0)),
        ],
        out_specs=pl.BlockSpec((1, s, gw), lambda bi, g: (bi, 0, g)),
        out_shape=jax.ShapeDtypeStruct((b, s, A_Q_HEADS * HEAD_DIM), jnp.bfloat16),
        compiler_params=_cparams(("parallel", "parallel")),
        name="attn_a",
    )(sink, p3, p3, p3, tbl)


def _attn_b_kernel(q_ref, k_ref, v_ref, tbl_ref, o_ref, qf, kf, vf,
                   acc0, acc1, acc2, m0, m1, m2, l0, l1, l2, *, seq):
    qb_rows = 2 * B_HALF
    kw = 4 * B_HALF

    def attend(qb, kb, vb, tb):
        s = lax.dot_general(qb, kb, _NT, preferred_element_type=jnp.float32)
        t = s * SCALE + tb
        m = jnp.max(t, axis=-1, keepdims=True)
        e = jnp.exp(t - m)
        l = jnp.sum(e, axis=-1, keepdims=True)
        a = jnp.dot(e.astype(jnp.bfloat16), vb, preferred_element_type=jnp.float32)
        wide = (qb.shape[0], HEAD_DIM)
        return a, jnp.broadcast_to(m, wide), jnp.broadcast_to(l, wide)

    def nat_block(qrow, krow, var):
        a, m, l = attend(q_ref[0, pl.ds(qrow, qb_rows), :], k_ref[0, pl.ds(krow, kw), :],
                         v_ref[0, pl.ds(krow, kw), :], tbl_ref[0, var])
        acc0[pl.ds(qrow, qb_rows), :] = a
        m0[pl.ds(qrow, qb_rows), :] = m
        l0[pl.ds(qrow, qb_rows), :] = l

    nblk = seq // qb_rows
    nat_block(0, 0, 0)

    def nat_body(i, carry):
        nat_block(pl.multiple_of(i * qb_rows, qb_rows), pl.multiple_of(i * qb_rows - B_HALF, B_HALF), 1)
        return carry

    lax.fori_loop(1, nblk - 1, nat_body, 0)
    nat_block(seq - qb_rows, seq - kw, 2)

    qf[...] = q_ref[0].astype(jnp.float32)
    kf[...] = k_ref[0].astype(jnp.float32)
    vf[...] = v_ref[0].astype(jnp.float32)

    def dilated(d, var0, acc, mm, ll):
        ln = seq // d
        nb = ln // qb_rows

        def body(r, carry):
            qr = qf[pl.ds(r, ln, stride=d), :].astype(jnp.bfloat16)
            kr = kf[pl.ds(r, ln, stride=d), :].astype(jnp.bfloat16)
            vr = vf[pl.ds(r, ln, stride=d), :].astype(jnp.bfloat16)
            for i in range(nb):
                if nb == 1:
                    k0, width, var = 0, ln, var0
                else:
                    k0 = min(max(i * qb_rows - B_HALF, 0), ln - kw)
                    width = kw
                    var = var0 + (0 if i == 0 else (2 if i == nb - 1 else 1))
                a, m, l = attend(qr[i * qb_rows:(i + 1) * qb_rows], kr[k0:k0 + width], vr[k0:k0 + width],
                                 tbl_ref[0, var, :, :width])
                rows = pl.ds(r + d * qb_rows * i, qb_rows, stride=d)
                acc[rows, :] = a
                mm[rows, :] = m
                ll[rows, :] = l
            return carry

        lax.fori_loop(0, d, body, 0)

    dilated(B_CONFIGS[1][1], 3, acc1, m1, l1)
    dilated(B_CONFIGS[2][1], 6, acc2, m2, l2)

    chunk = 256

    def merge(c, carry):
        rows = pl.ds(pl.multiple_of(c * chunk, chunk), chunk)
        ma, mb, mc = m0[rows, :], m1[rows, :], m2[rows, :]
        mx = jnp.maximum(jnp.maximum(ma, mb), mc)
        wa, wb, wc = jnp.exp(ma - mx), jnp.exp(mb - mx), jnp.exp(mc - mx)
        num = wa * acc0[rows, :] + wb * acc1[rows, :] + wc * acc2[rows, :]
        den = wa * l0[rows, :] + wb * l1[rows, :] + wc * l2[rows, :]
        o_ref[0, rows, :] = (num / den).astype(o_ref.dtype)
        return carry

    lax.fori_loop(0, seq // chunk, merge, 0)


def _attn_b(p3, tbl):
    b, s, _ = p3.shape
    f32_slab = pltpu.VMEM((s, HEAD_DIM), jnp.float32)
    return pl.pallas_call(
        functools.partial(_attn_b_kernel, seq=s),
        grid=(b, B_HEADS),
        in_specs=[
            pl.BlockSpec((1, s, HEAD_DIM), lambda bi, h: (bi, 0, B_Q_COL + h)),
            pl.BlockSpec((1, s, HEAD_DIM), lambda bi, h: (bi, 0, B_K_COL + h)),
            pl.BlockSpec((1, s, HEAD_DIM), lambda bi, h: (bi, 0, B_V_COL + h)),
            pl.BlockSpec((1,) + tbl.shape[1:], lambda bi, h: (h, 0, 0, 0)),
        ],
        out_specs=pl.BlockSpec((1, s, HEAD_DIM), lambda bi, h: (bi, 0, h)),
        out_shape=jax.ShapeDtypeStruct((b, s, B_HEADS * HEAD_DIM), jnp.bfloat16),
        scratch_shapes=[f32_slab] * 12,
        compiler_params=_cparams(("parallel", "parallel")),
        name="attn_b",
    )(p3, p3, p3, tbl)


def _attn_c_kernel(q_ref, k_ref, v_ref, tbl_ref, o_ref, *, seq):
    nq = NA_QROWS * GRID_W
    nk = NA_KROWS * GRID_W
    n_groups = seq // nq

    def group(qrow, krow, var):
        qb = q_ref[0, pl.ds(qrow, nq), :]
        kb = k_ref[0, pl.ds(krow, nk), :]
        vb = v_ref[0, pl.ds(krow, nk), :]
        s = lax.dot_general(qb, kb, _NT, preferred_element_type=jnp.float32)
        t = s * SCALE + tbl_ref[0, var]
        m = jnp.max(t, axis=-1, keepdims=True)
        e = jnp.exp(t - m)
        l = jnp.sum(e, axis=-1, keepdims=True)
        o = jnp.dot(e.astype(jnp.bfloat16), vb, preferred_element_type=jnp.float32)
        o_ref[0, pl.ds(qrow, nq), :] = (o / l).astype(o_ref.dtype)

    group(0, 0, 0)

    def body(g, carry):
        group(pl.multiple_of(g * nq, nq), pl.multiple_of((g - 1) * nq, nq), 1)
        return carry

    lax.fori_loop(1, n_groups - 1, body, 0)
    group(seq - nq, seq - nk, 2)


def _attn_c(p3, tbl):
    b, s, _ = p3.shape
    return pl.pallas_call(
        functools.partial(_attn_c_kernel, seq=s),
        grid=(b, C_HEADS),
        in_specs=[
            pl.BlockSpec((1, s, HEAD_DIM), lambda bi, h: (bi, 0, h)),
            pl.BlockSpec((1, s, HEAD_DIM), lambda bi, h: (bi, 0, C_HEADS + h)),
            pl.BlockSpec((1, s, HEAD_DIM), lambda bi, h: (bi, 0, 2 * C_HEADS + h)),
            pl.BlockSpec((1,) + tbl.shape[1:], lambda bi, h: (h, 0, 0, 0)),
        ],
        out_specs=pl.BlockSpec((1, s, HEAD_DIM), lambda bi, h: (bi, 0, h)),
        out_shape=jax.ShapeDtypeStruct((b, s, C_HEADS * HEAD_DIM), jnp.bfloat16),
        compiler_params=_cparams(("parallel", "parallel")),
        name="attn_c",
    )(p3, p3, p3, tbl)


def _gate_out_kernel(*refs, n_y, n_z, final_norm):
    y_refs = refs[:n_y]
    z_refs = refs[n_y:n_y + n_z]
    w_ref, x_ref = refs[n_y + n_z:n_y + n_z + 2]
    o_ref = refs[-1]
    y = jnp.concatenate([r[...] for r in y_refs], axis=1).astype(jnp.float32)
    z = jnp.concatenate([r[...] for r in z_refs], axis=1).astype(jnp.float32)
    gated = (y * (z / (1.0 + jnp.exp(-z)))).astype(jnp.bfloat16)
    out = x_ref[...] + jnp.dot(gated, w_ref[...], preferred_element_type=jnp.float32)
    if final_norm:
        g_ref = refs[-2]
        ms = jnp.mean(out * out, axis=-1, keepdims=True)
        out = out * lax.rsqrt(ms + RMS_EPS) * g_ref[...]
    o_ref[...] = out


def _gate_out(ys, p2d, z_col, z_blocks, w, x2d, g_final, *, tm, name):
    m, d = x2d.shape
    zw = d // z_blocks
    zb0 = z_col * HEAD_DIM // zw
    assert zb0 * zw == z_col * HEAD_DIM
    in_specs = [pl.BlockSpec((tm, y.shape[1]), lambda i: (i, 0)) for y in ys]
    in_specs += [pl.BlockSpec((tm, zw), functools.partial(lambda i, c: (i, c), c=zb0 + c)) for c in range(z_blocks)]
    in_specs += [pl.BlockSpec(w.shape, lambda i: (0, 0)), pl.BlockSpec((tm, d), lambda i: (i, 0))]
    args = list(ys) + [p2d] * z_blocks + [w, x2d]
    if g_final is not None:
        in_specs.append(pl.BlockSpec((1, d), lambda i: (0, 0)))
        args.append(g_final.reshape(1, d))
    return pl.pallas_call(
        functools.partial(_gate_out_kernel, n_y=len(ys), n_z=z_blocks, final_norm=g_final is not None),
        grid=(m // tm,),
        in_specs=in_specs,
        out_specs=pl.BlockSpec((tm, d), lambda i: (i, 0)),
        out_shape=jax.ShapeDtypeStruct((m, d), jnp.float32),
        compiler_params=_cparams(("parallel",)),
        name=name,
    )(*args)


def kernel(x, ln_ab, w_in_ab, sink_a, w_out_ab, ln_c, w_in_c, rpb_c, w_out_c, ln_f):
    b, s, d = x.shape
    assert s == 2048 and d == 2048, "tiling below is written for SEQ = D_MODEL = 2048"
    assert ln_ab.shape[0] == 1 and ln_c.shape[0] == 1, "depth-2 trunk: one layer of each kind"
    m = b * s
    bf16 = jnp.bfloat16
    x2d = x.reshape(m, d)

    slopes = _alibi_slopes(A_Q_HEADS + B_HEADS)
    tbl_a = _band_bias(A_HALF_WINDOW, 3 * A_HALF_WINDOW, A_HALF_WINDOW, A_HALF_WINDOW, 1, slopes[:A_Q_HEADS])
    qb, kw = 2 * B_HALF, 4 * B_HALF
    tbl_b = jnp.stack(
        [_band_bias(qb, kw, off, B_HALF, dil, slopes[A_Q_HEADS:]) for dil in (1, 4) for off in (0, B_HALF, 2 * B_HALF)]
        + [_band_bias(qb, kw, 0, B_HALF, B_CONFIGS[2][1], slopes[A_Q_HEADS:])], axis=1)
    tbl_c = _na_bias_tables(rpb_c[0])

    p = _norm_proj(x2d, ln_ab[0], w_in_ab[0].astype(bf16), tm=512, tn=3328, name="norm_proj_ab")
    p3 = p.reshape(b, s, p.shape[1])
    ya = _attn_a(p3, sink_a[0], tbl_a)
    yb = _attn_b(p3, tbl_b)
    x1 = _gate_out([ya.reshape(m, -1), yb.reshape(m, -1)], p, Z_AB_COL, 4, w_out_ab[0].astype(bf16), x2d, None,
                   tm=512, name="gate_out_ab")

    pc = _norm_proj(x1, ln_c[0], w_in_c[0].astype(bf16), tm=1024, tn=2048, name="norm_proj_c")
    yc = _attn_c(pc.reshape(b, s, pc.shape[1]), tbl_c)
    out = _gate_out([yc.reshape(m, -1)], pc, 3 * C_HEADS, 1, w_out_c[0].astype(bf16), x1, ln_f,
                    tm=512, name="gate_out_c")
    return out.reshape(b, s, d)
```

```python
import functools

import numpy as np
import jax
import jax.numpy as jnp
from jax import lax
from jax.experimental import pallas as pl
from jax.experimental.pallas import tpu as pltpu

HEAD_DIM = 128
A_Q_HEADS = 8
A_KV_HEADS = 2
A_GROUP = A_Q_HEADS // A_KV_HEADS
A_HALF_WINDOW = 128
B_HEADS = 8
B_CONFIGS = ((128, 1), (512, 4), (2048, 16))
B_HALF = 64
C_HEADS = 16
GRID_W = 64
NA_KH = 8
NA_KW = 16
RMS_EPS = 1e-5
NEG = -1e30
LOG2E = 1.4426950408889634
QK_SCALE = HEAD_DIM ** -0.5 * LOG2E

A_Q_COL = 0
A_K_COL = A_Q_HEADS
A_V_COL = A_K_COL + A_KV_HEADS
B_Q_COL = A_V_COL + A_KV_HEADS
B_K_COL = B_Q_COL + B_HEADS
B_V_COL = B_K_COL + B_HEADS
Z_AB_COL = B_V_COL + B_HEADS

NA_QROWS = 2
NA_KROWS = NA_QROWS + NA_KH - 1

A_UNROLL = 4
B_UNROLL = 16

VMEM_LIMIT = 56 * 1024 * 1024

_NT = (((1,), (1,)), ((), ()))


def _cparams(sem):
    return pltpu.CompilerParams(dimension_semantics=sem, vmem_limit_bytes=VMEM_LIMIT)


def _edge_variant(i, n):
    return jnp.where(i == 0, 0, jnp.where(i == n - 1, 2, 1))


def _with_ones(vb):
    return jnp.concatenate([vb, jnp.ones_like(vb)], axis=1)


def _norm_proj_kernel(x_ref, g_ref, w_ref, cs_ref, o_ref, hn_ref):
    @pl.when(pl.program_id(1) == 0)
    def _():
        x = x_ref[...]
        ms = jnp.mean(x * x, axis=-1, keepdims=True)
        hn_ref[...] = (x * lax.rsqrt(ms + RMS_EPS) * g_ref[...]).astype(hn_ref.dtype)

    acc = jnp.dot(hn_ref[...], w_ref[...], preferred_element_type=jnp.float32)
    o_ref[...] = (acc * cs_ref[...]).astype(o_ref.dtype)


def _norm_proj(x2d, g, w, q_cols, *, tm, tn, name):
    m, d = x2d.shape
    n = w.shape[1]
    cs = np.ones((1, n), np.float32)
    for lo, hi in q_cols:
        cs[:, lo:hi] = QK_SCALE
    return pl.pallas_call(
        _norm_proj_kernel,
        grid=(m // tm, n // tn),
        in_specs=[
            pl.BlockSpec((tm, d), lambda i, j: (i, 0)),
            pl.BlockSpec((1, d), lambda i, j: (0, 0)),
            pl.BlockSpec((d, tn), lambda i, j: (0, j)),
            pl.BlockSpec((1, tn), lambda i, j: (0, j)),
        ],
        out_specs=pl.BlockSpec((tm, tn), lambda i, j: (i, j)),
        out_shape=jax.ShapeDtypeStruct((m, n), jnp.bfloat16),
        scratch_shapes=[pltpu.VMEM((tm, d), jnp.bfloat16)],
        compiler_params=_cparams(("parallel", "arbitrary")),
        name=name,
    )(x2d, g.reshape(1, d), w, jnp.asarray(cs))


def _alibi_slopes(n):
    return jnp.exp2(-8.0 * jnp.arange(1, n + 1, dtype=jnp.float32) / n)


def _band_bias(nq, nk, off, half, dist_scale, slopes):
    rel = np.arange(nk)[None, :] - np.arange(nq)[:, None] - off
    valid = np.abs(rel) <= half
    dist = jnp.asarray((np.abs(rel) * dist_scale).astype(np.float32))
    bias = -slopes[:, None, None] * dist
    return jnp.where(valid[None], bias * LOG2E, NEG)


def _na_groups(rows):
    starts, variant_of, variants = [], [], []
    for g in range(rows // NA_QROWS):
        ws = int(np.clip(g * NA_QROWS - NA_KH // 2, 0, rows - NA_KROWS))
        sig = []
        for rr in range(NA_QROWS):
            r = g * NA_QROWS + rr
            r0 = int(np.clip(r - NA_KH // 2, 0, rows - NA_KH))
            assert ws <= r0 and r0 + NA_KH <= ws + NA_KROWS
            sig.append((r - ws, r0 - ws))
        sig = tuple(sig)
        if sig not in variants:
            variants.append(sig)
        starts.append(ws)
        variant_of.append(variants.index(sig))
    return starts, variant_of, variants


def _na_bias_tables(rpb, variants):
    h = rpb.shape[0]
    cq = np.arange(GRID_W)
    c0 = np.clip(cq - NA_KW // 2, 0, GRID_W - NA_KW)
    ck = np.arange(GRID_W)
    col_valid = (ck[None, :] >= c0[:, None]) & (ck[None, :] < c0[:, None] + NA_KW)
    pad = GRID_W - NA_KW
    rp = jnp.pad(rpb * LOG2E, ((0, 0), (0, 0), (pad, pad)))
    a = jnp.stack([rp[:, :, GRID_W - 1 - q:2 * GRID_W - 1 - q] for q in range(GRID_W)], axis=2)
    a = jnp.where(col_valid[None, None], a, NEG)
    tabs = []
    for sig in variants:
        blocks = []
        for r, r0 in sig:
            per_wr = []
            for kr in range(NA_KROWS):
                if r0 <= kr < r0 + NA_KH:
                    per_wr.append(a[:, kr - r + NA_KH - 1])
                else:
                    per_wr.append(jnp.full((h, GRID_W, GRID_W), NEG, jnp.float32))
            blocks.append(jnp.stack(per_wr, axis=2))
        tabs.append(jnp.stack(blocks, axis=1).reshape(h, NA_QROWS * GRID_W, NA_KROWS * GRID_W))
    return jnp.stack(tabs, axis=1)


def _attn_a_kernel(sink_ref, q_ref, k_ref, v_ref, tbl_ref, o_ref, *, seq):
    blk = A_HALF_WINDOW
    nk = 3 * blk
    nblk = seq // blk
    g = pl.program_id(1)
    sinks = [sink_ref[g * A_GROUP + j] * LOG2E for j in range(A_GROUP)]

    def block(i):
        qrow = pl.multiple_of(i * blk, blk)
        krow = pl.multiple_of(jnp.clip((i - 1) * blk, 0, seq - nk), blk)
        var = _edge_variant(i, nblk)
        q4 = q_ref[0, pl.ds(qrow, blk), :]
        qs = jnp.concatenate([q4[:, j * HEAD_DIM:(j + 1) * HEAD_DIM] for j in range(A_GROUP)], axis=0)
        kb = k_ref[0, pl.ds(krow, nk), :]
        vb = v_ref[0, pl.ds(krow, nk), :]
        s = lax.dot_general(qs, kb, _NT, preferred_element_type=jnp.float32)
        es, ls = [], []
        for j in range(A_GROUP):
            t = s[j * blk:(j + 1) * blk, :] + tbl_ref[j, var]
            m = jnp.maximum(jnp.max(t, axis=-1, keepdims=True), sinks[j])
            e = jnp.exp2(t - m)
            ls.append(jnp.sum(e, axis=-1, keepdims=True) + jnp.exp2(sinks[j] - m))
            es.append(e.astype(jnp.bfloat16))
        o = jnp.dot(jnp.concatenate(es, axis=0), vb, preferred_element_type=jnp.float32)
        out = jnp.concatenate([o[j * blk:(j + 1) * blk] / ls[j] for j in range(A_GROUP)], axis=1)
        o_ref[0, pl.ds(qrow, blk), :] = out.astype(o_ref.dtype)

    def body(it, carry):
        for u in range(A_UNROLL):
            block(it * A_UNROLL + u)
        return carry

    lax.fori_loop(0, nblk // A_UNROLL, body, 0)


def _attn_a(p3, sink, tbl):
    b, s, _ = p3.shape
    gw = A_GROUP * HEAD_DIM
    return pl.pallas_call(
        functools.partial(_attn_a_kernel, seq=s),
        grid=(b, A_KV_HEADS),
        in_specs=[
            pl.BlockSpec(memory_space=pltpu.SMEM),
            pl.BlockSpec((1, s, gw), lambda bi, g: (bi, 0, g)),
            pl.BlockSpec((1, s, HEAD_DIM), lambda bi, g: (bi, 0, A_K_COL + g)),
            pl.BlockSpec((1, s, HEAD_DIM), lambda bi, g: (bi, 0, A_V_COL + g)),
            pl.BlockSpec((A_GROUP,) + tbl.shape[1:], lambda bi, g: (g, 0, 0, 0)),
        ],
        out_specs=pl.BlockSpec((1, s, gw), lambda bi, g: (bi, 0, g)),
        out_shape=jax.ShapeDtypeStruct((b, s, A_Q_HEADS * HEAD_DIM), jnp.bfloat16),
        compiler_params=_cparams(("parallel", "parallel")),
        name="attn_a",
    )(sink, p3, p3, p3, tbl)


def _attn_b_kernel(q_ref, k_ref, v_ref, tbl_ref, o_ref, qf, kf, vf,
                   acc0, acc1, acc2, m0, m1, m2, l0, l1, l2, *, seq):
    qb_rows = 2 * B_HALF
    kw = 4 * B_HALF

    def attend(qb, kb, vb, tb):
        s = lax.dot_general(qb, kb, _NT, preferred_element_type=jnp.float32)
        t = s + tb
        m = jnp.max(t, axis=-1, keepdims=True)
        e = jnp.exp2(t - m).astype(jnp.bfloat16)
        al = jnp.dot(e, _with_ones(vb), preferred_element_type=jnp.float32)
        return al[:, :HEAD_DIM], jnp.broadcast_to(m, (qb.shape[0], HEAD_DIM)), al[:, HEAD_DIM:]

    nblk = seq // qb_rows

    def nat_block(i):
        qrow = pl.multiple_of(i * qb_rows, qb_rows)
        krow = pl.multiple_of(jnp.clip(i * qb_rows - B_HALF, 0, seq - kw), B_HALF)
        a, m, l = attend(q_ref[0, pl.ds(qrow, qb_rows), :], k_ref[0, pl.ds(krow, kw), :],
                         v_ref[0, pl.ds(krow, kw), :], tbl_ref[0, _edge_variant(i, nblk)])
        acc0[pl.ds(qrow, qb_rows), :] = a
        m0[pl.ds(qrow, qb_rows), :] = m
        l0[pl.ds(qrow, qb_rows), :] = l

    def nat_body(it, carry):
        for u in range(B_UNROLL):
            nat_block(it * B_UNROLL + u)
        return carry

    lax.fori_loop(0, nblk // B_UNROLL, nat_body, 0)

    qf[...] = q_ref[0].astype(jnp.float32)
    kf[...] = k_ref[0].astype(jnp.float32)
    vf[...] = v_ref[0].astype(jnp.float32)

    def dilated(d, var0, acc, mm, ll):
        ln = seq // d
        nb = ln // qb_rows
        per_iter = max(B_UNROLL // nb, 1)

        def residue(r):
            qr = qf[pl.ds(r, ln, stride=d), :].astype(jnp.bfloat16)
            kr = kf[pl.ds(r, ln, stride=d), :].astype(jnp.bfloat16)
            vr = vf[pl.ds(r, ln, stride=d), :].astype(jnp.bfloat16)
            for i in range(nb):
                if nb == 1:
                    k0, width, var = 0, ln, var0
                else:
                    k0 = min(max(i * qb_rows - B_HALF, 0), ln - kw)
                    width = kw
                    var = var0 + (0 if i == 0 else (2 if i == nb - 1 else 1))
                a, m, l = attend(qr[i * qb_rows:(i + 1) * qb_rows], kr[k0:k0 + width], vr[k0:k0 + width],
                                 tbl_ref[0, var, :, :width])
                rows = pl.ds(r + d * qb_rows * i, qb_rows, stride=d)
                acc[rows, :] = a
                mm[rows, :] = m
                ll[rows, :] = l

        def body(it, carry):
            for u in range(per_iter):
                residue(it * per_iter + u)
            return carry

        lax.fori_loop(0, d // per_iter, body, 0)

    dilated(B_CONFIGS[1][1], 3, acc1, m1, l1)
    dilated(B_CONFIGS[2][1], 6, acc2, m2, l2)

    merge_rows = 2 * qb_rows

    def merge(c, carry):
        rows = pl.ds(pl.multiple_of(c * merge_rows, merge_rows), merge_rows)
        ma, mb, mc = m0[rows, :], m1[rows, :], m2[rows, :]
        mx = jnp.maximum(jnp.maximum(ma, mb), mc)
        wa, wb, wc = jnp.exp2(ma - mx), jnp.exp2(mb - mx), jnp.exp2(mc - mx)
        num = wa * acc0[rows, :] + wb * acc1[rows, :] + wc * acc2[rows, :]
        den = wa * l0[rows, :] + wb * l1[rows, :] + wc * l2[rows, :]
        o_ref[0, rows, :] = (num / den).astype(o_ref.dtype)
        return carry

    lax.fori_loop(0, seq // merge_rows, merge, 0)


def _attn_b(p3, tbl):
    b, s, _ = p3.shape
    f32_slab = pltpu.VMEM((s, HEAD_DIM), jnp.float32)
    return pl.pallas_call(
        functools.partial(_attn_b_kernel, seq=s),
        grid=(B_HEADS, b),
        in_specs=[
            pl.BlockSpec((1, s, HEAD_DIM), lambda h, bi: (bi, 0, B_Q_COL + h)),
            pl.BlockSpec((1, s, HEAD_DIM), lambda h, bi: (bi, 0, B_K_COL + h)),
            pl.BlockSpec((1, s, HEAD_DIM), lambda h, bi: (bi, 0, B_V_COL + h)),
            pl.BlockSpec((1,) + tbl.shape[1:], lambda h, bi: (h, 0, 0, 0)),
        ],
        out_specs=pl.BlockSpec((1, s, HEAD_DIM), lambda h, bi: (bi, 0, h)),
        out_shape=jax.ShapeDtypeStruct((b, s, B_HEADS * HEAD_DIM), jnp.bfloat16),
        scratch_shapes=[f32_slab] * 12,
        compiler_params=_cparams(("parallel", "parallel")),
        name="attn_b",
    )(p3, p3, p3, tbl)


def _attn_c_kernel(q_ref, k_ref, v_ref, tbl_ref, o_ref, *, starts, variant_of):
    nq = NA_QROWS * GRID_W
    nk = NA_KROWS * GRID_W
    for g, (ws, var) in enumerate(zip(starts, variant_of)):
        qb = q_ref[0, g * nq:(g + 1) * nq, :]
        kb = k_ref[0, ws * GRID_W:ws * GRID_W + nk, :]
        vb = v_ref[0, ws * GRID_W:ws * GRID_W + nk, :]
        s = lax.dot_general(qb, kb, _NT, preferred_element_type=jnp.float32)
        t = s + tbl_ref[0, var]
        m = jnp.max(t, axis=-1, keepdims=True)
        e = jnp.exp2(t - m).astype(jnp.bfloat16)
        ol = jnp.dot(e, _with_ones(vb), preferred_element_type=jnp.float32)
        o_ref[0, g * nq:(g + 1) * nq, :] = (ol[:, :HEAD_DIM] / ol[:, HEAD_DIM:]).astype(o_ref.dtype)


def _attn_c(p3, tbl, starts, variant_of):
    b, s, _ = p3.shape
    return pl.pallas_call(
        functools.partial(_attn_c_kernel, starts=starts, variant_of=variant_of),
        grid=(C_HEADS, b),
        in_specs=[
            pl.BlockSpec((1, s, HEAD_DIM), lambda h, bi: (bi, 0, h)),
            pl.BlockSpec((1, s, HEAD_DIM), lambda h, bi: (bi, 0, C_HEADS + h)),
            pl.BlockSpec((1, s, HEAD_DIM), lambda h, bi: (bi, 0, 2 * C_HEADS + h)),
            pl.BlockSpec((1,) + tbl.shape[1:], lambda h, bi: (h, 0, 0, 0)),
        ],
        out_specs=pl.BlockSpec((1, s, HEAD_DIM), lambda h, bi: (bi, 0, h)),
        out_shape=jax.ShapeDtypeStruct((b, s, C_HEADS * HEAD_DIM), jnp.bfloat16),
        compiler_params=_cparams(("parallel", "parallel")),
        name="attn_c",
    )(p3, p3, p3, tbl)


def _gate_out_kernel(*refs, n_y, n_z, final_norm):
    y_refs = refs[:n_y]
    z_refs = refs[n_y:n_y + n_z]
    w_ref, x_ref = refs[n_y + n_z:n_y + n_z + 2]
    o_ref = refs[-1]
    y = jnp.concatenate([r[...] for r in y_refs], axis=1).astype(jnp.float32)
    z = jnp.concatenate([r[...] for r in z_refs], axis=1).astype(jnp.float32)
    gated = (y * (z / (1.0 + jnp.exp(-z)))).astype(jnp.bfloat16)
    out = x_ref[...] + jnp.dot(gated, w_ref[...], preferred_element_type=jnp.float32)
    if final_norm:
        g_ref = refs[-2]
        ms = jnp.mean(out * out, axis=-1, keepdims=True)
        out = out * lax.rsqrt(ms + RMS_EPS) * g_ref[...]
    o_ref[...] = out


def _gate_out(ys, p2d, z_col, z_blocks, w, x2d, g_final, *, tm, name):
    m, d = x2d.shape
    zw = d // z_blocks
    zb0 = z_col * HEAD_DIM // zw
    assert zb0 * zw == z_col * HEAD_DIM
    in_specs = [pl.BlockSpec((tm, y.shape[1]), lambda i: (i, 0)) for y in ys]
    in_specs += [pl.BlockSpec((tm, zw), functools.partial(lambda i, c: (i, c), c=zb0 + c)) for c in range(z_blocks)]
    in_specs += [pl.BlockSpec(w.shape, lambda i: (0, 0)), pl.BlockSpec((tm, d), lambda i: (i, 0))]
    args = list(ys) + [p2d] * z_blocks + [w, x2d]
    if g_final is not None:
        in_specs.append(pl.BlockSpec((1, d), lambda i: (0, 0)))
        args.append(g_final.reshape(1, d))
    return pl.pallas_call(
        functools.partial(_gate_out_kernel, n_y=len(ys), n_z=z_blocks, final_norm=g_final is not None),
        grid=(m // tm,),
        in_specs=in_specs,
        out_specs=pl.BlockSpec((tm, d), lambda i: (i, 0)),
        out_shape=jax.ShapeDtypeStruct((m, d), jnp.float32),
        compiler_params=_cparams(("parallel",)),
        name=name,
    )(*args)


def kernel(x, ln_ab, w_in_ab, sink_a, w_out_ab, ln_c, w_in_c, rpb_c, w_out_c, ln_f):
    b, s, d = x.shape
    assert s == 2048 and d == 2048, "tiling below is written for SEQ = D_MODEL = 2048"
    assert ln_ab.shape[0] == 1 and ln_c.shape[0] == 1, "depth-2 trunk: one layer of each kind"
    m = b * s
    bf16 = jnp.bfloat16
    x2d = x.reshape(m, d)

    slopes = _alibi_slopes(A_Q_HEADS + B_HEADS)
    aw = A_HALF_WINDOW
    tbl_a = jnp.stack([_band_bias(aw, 3 * aw, off, aw, 1, slopes[:A_Q_HEADS]) for off in (0, aw, 2 * aw)],
                      axis=1)
    qb, kw = 2 * B_HALF, 4 * B_HALF
    tbl_b = jnp.stack(
        [_band_bias(qb, kw, off, B_HALF, dil, slopes[A_Q_HEADS:]) for dil in (1, 4) for off in (0, B_HALF, 2 * B_HALF)]
        + [_band_bias(qb, kw, 0, B_HALF, B_CONFIGS[2][1], slopes[A_Q_HEADS:])], axis=1)
    na_starts, na_variant_of, na_variants = _na_groups(s // GRID_W)
    tbl_c = _na_bias_tables(rpb_c[0], na_variants)

    hd = HEAD_DIM
    q_cols_ab = ((A_Q_COL * hd, A_K_COL * hd), (B_Q_COL * hd, B_K_COL * hd))
    p = _norm_proj(x2d, ln_ab[0], w_in_ab[0].astype(bf16), q_cols_ab, tm=512, tn=3328, name="norm_proj_ab")
    p3 = p.reshape(b, s, p.shape[1])
    ya = _attn_a(p3, sink_a[0], tbl_a)
    yb = _attn_b(p3, tbl_b)
    x1 = _gate_out([ya.reshape(m, -1), yb.reshape(m, -1)], p, Z_AB_COL, 4, w_out_ab[0].astype(bf16), x2d, None,
                   tm=512, name="gate_out_ab")

    pc = _norm_proj(x1, ln_c[0], w_in_c[0].astype(bf16), ((0, C_HEADS * hd),), tm=1024, tn=2048, name="norm_proj_c")
    yc = _attn_c(pc.reshape(b, s, pc.shape[1]), tbl_c, na_starts, na_variant_of)
    out = _gate_out([yc.reshape(m, -1)], pc, 3 * C_HEADS, 1, w_out_c[0].astype(bf16), x1, ln_f,
                    tm=512, name="gate_out_c")
    return out.reshape(b, s, d)
```

```python
import functools
import math

import numpy as np
import jax
import jax.numpy as jnp
from jax import lax
from jax.experimental import pallas as pl
from jax.experimental.pallas import tpu as pltpu

HEAD_DIM = 128
A_Q_HEADS = 8
A_KV_HEADS = 2
A_GROUP = A_Q_HEADS // A_KV_HEADS
A_HALF_WINDOW = 128
B_HEADS = 8
B_CONFIGS = ((128, 1), (512, 4), (2048, 16))
B_HALF = 64
C_HEADS = 16
GRID_W = 64
NA_KH = 8
NA_KW = 16
RMS_EPS = 1e-5
NEG = -1e30
LOG2E = 1.4426950408889634
QK_SCALE = HEAD_DIM ** -0.5 * LOG2E

A_Q_COL = 0
A_K_COL = A_Q_HEADS
A_V_COL = A_K_COL + A_KV_HEADS
B_Q_COL = A_V_COL + A_KV_HEADS
B_K_COL = B_Q_COL + B_HEADS
B_V_COL = B_K_COL + B_HEADS
Z_AB_COL = B_V_COL + B_HEADS

NA_QROWS = 2
NA_KROWS = NA_QROWS + NA_KH - 1

A_UNROLL = 4
B_UNROLL = 16
ATTN_BATCH = 4

VMEM_LIMIT = 56 * 1024 * 1024

_NT = (((1,), (1,)), ((), ()))


def _cparams(sem):
    return pltpu.CompilerParams(dimension_semantics=sem, vmem_limit_bytes=VMEM_LIMIT)


def _edge_variant(i, n):
    return jnp.where(i == 0, 0, jnp.where(i == n - 1, 2, 1))


def _with_ones(vb):
    return jnp.concatenate([vb, jnp.ones_like(vb)], axis=1)


def _norm_proj_kernel(x_ref, g_ref, w_ref, cs_ref, o_ref, hn_ref):
    @pl.when(pl.program_id(1) == 0)
    def _():
        x = x_ref[...]
        ms = jnp.mean(x * x, axis=-1, keepdims=True)
        hn_ref[...] = (x * lax.rsqrt(ms + RMS_EPS) * g_ref[...]).astype(hn_ref.dtype)

    acc = jnp.dot(hn_ref[...], w_ref[...], preferred_element_type=jnp.float32)
    o_ref[...] = (acc * cs_ref[...]).astype(o_ref.dtype)


def _norm_proj(x2d, g, w, q_cols, *, tm, tn, name):
    m, d = x2d.shape
    n = w.shape[1]
    cs = np.ones((1, n), np.float32)
    for lo, hi in q_cols:
        cs[:, lo:hi] = QK_SCALE
    return pl.pallas_call(
        _norm_proj_kernel,
        grid=(m // tm, n // tn),
        in_specs=[
            pl.BlockSpec((tm, d), lambda i, j: (i, 0)),
            pl.BlockSpec((1, d), lambda i, j: (0, 0)),
            pl.BlockSpec((d, tn), lambda i, j: (0, j)),
            pl.BlockSpec((1, tn), lambda i, j: (0, j)),
        ],
        out_specs=pl.BlockSpec((tm, tn), lambda i, j: (i, j)),
        out_shape=jax.ShapeDtypeStruct((m, n), jnp.bfloat16),
        scratch_shapes=[pltpu.VMEM((tm, d), jnp.bfloat16)],
        compiler_params=_cparams(("parallel", "arbitrary")),
        name=name,
    )(x2d, g.reshape(1, d), w, jnp.asarray(cs))


def _alibi_slopes(n):
    return jnp.exp2(-8.0 * jnp.arange(1, n + 1, dtype=jnp.float32) / n)


def _band_bias(nq, nk, off, half, dist_scale, slopes):
    rel = np.arange(nk)[None, :] - np.arange(nq)[:, None] - off
    valid = np.abs(rel) <= half
    dist = jnp.asarray((np.abs(rel) * dist_scale).astype(np.float32))
    bias = -slopes[:, None, None] * dist
    return jnp.where(valid[None], bias * LOG2E, NEG)


def _na_groups(rows):
    starts, variant_of, variants = [], [], []
    for g in range(rows // NA_QROWS):
        ws = int(np.clip(g * NA_QROWS - NA_KH // 2, 0, rows - NA_KROWS))
        sig = []
        for rr in range(NA_QROWS):
            r = g * NA_QROWS + rr
            r0 = int(np.clip(r - NA_KH // 2, 0, rows - NA_KH))
            assert ws <= r0 and r0 + NA_KH <= ws + NA_KROWS
            sig.append((r - ws, r0 - ws))
        sig = tuple(sig)
        if sig not in variants:
            variants.append(sig)
        starts.append(ws)
        variant_of.append(variants.index(sig))
    return starts, variant_of, variants


def _na_bias_tables(rpb, variants):
    h, n_dr, n_dc = rpb.shape
    cq = np.arange(GRID_W)
    c0 = np.clip(cq - NA_KW // 2, 0, GRID_W - NA_KW)
    ck = np.arange(GRID_W)
    col_valid = (ck[None, :] >= c0[:, None]) & (ck[None, :] < c0[:, None] + NA_KW)
    dc = ck[None, :] - cq[:, None] + NA_KW - 1
    onehot = ((dc[None] == np.arange(n_dc)[:, None, None]) & col_valid[None]).astype(np.float32)
    a = jnp.einsum("hdx,xqk->hdqk", rpb * LOG2E, jnp.asarray(onehot), precision=lax.Precision.HIGHEST)
    a = jnp.where(col_valid[None, None], a, NEG)
    a = jnp.concatenate([a, jnp.full((h, 1, GRID_W, GRID_W), NEG, jnp.float32)], axis=1)
    idx = np.full((len(variants), NA_QROWS, NA_KROWS), n_dr, np.int32)
    for v, sig in enumerate(variants):
        for rr, (r, r0) in enumerate(sig):
            for kr in range(r0, r0 + NA_KH):
                idx[v, rr, kr] = kr - r + NA_KH - 1
    t = jnp.take(a, jnp.asarray(idx.reshape(-1)), axis=1)
    t = t.reshape(h, len(variants), NA_QROWS, NA_KROWS, GRID_W, GRID_W).transpose(0, 1, 2, 4, 3, 5)
    return t.reshape(h, len(variants), NA_QROWS * GRID_W, NA_KROWS * GRID_W)


def _attn_a_kernel(sink_ref, q_ref, k_ref, v_ref, tbl_ref, o_ref, *, seq):
    blk = A_HALF_WINDOW
    nk = 3 * blk
    nblk = seq // blk
    g = pl.program_id(0)
    sinks = [sink_ref[g * A_GROUP + j] * LOG2E for j in range(A_GROUP)]

    def block(bb, i):
        qrow = pl.multiple_of(i * blk, blk)
        krow = pl.multiple_of(jnp.clip((i - 1) * blk, 0, seq - nk), blk)
        var = _edge_variant(i, nblk)
        q4 = q_ref[bb, pl.ds(qrow, blk), :]
        qs = jnp.concatenate([q4[:, j * HEAD_DIM:(j + 1) * HEAD_DIM] for j in range(A_GROUP)], axis=0)
        kb = k_ref[bb, pl.ds(krow, nk), :]
        vb = v_ref[bb, pl.ds(krow, nk), :]
        s = lax.dot_general(qs, kb, _NT, preferred_element_type=jnp.float32)
        es, ls = [], []
        for j in range(A_GROUP):
            t = s[j * blk:(j + 1) * blk, :] + tbl_ref[j, var]
            m = jnp.maximum(jnp.max(t, axis=-1, keepdims=True), sinks[j])
            e = jnp.exp2(t - m)
            ls.append(jnp.sum(e, axis=-1, keepdims=True) + jnp.exp2(sinks[j] - m))
            es.append(e.astype(jnp.bfloat16))
        o = jnp.dot(jnp.concatenate(es, axis=0), vb, preferred_element_type=jnp.float32)
        out = jnp.concatenate([o[j * blk:(j + 1) * blk] / ls[j] for j in range(A_GROUP)], axis=1)
        o_ref[bb, pl.ds(qrow, blk), :] = out.astype(o_ref.dtype)

    iters = nblk // A_UNROLL

    def body(n, carry):
        bb, it = n // iters, n % iters
        for u in range(A_UNROLL):
            block(bb, it * A_UNROLL + u)
        return carry

    lax.fori_loop(0, q_ref.shape[0] * iters, body, 0)


def _attn_a(p3, sink, tbl):
    b, s, _ = p3.shape
    gw = A_GROUP * HEAD_DIM
    nb = math.gcd(b, ATTN_BATCH)
    return pl.pallas_call(
        functools.partial(_attn_a_kernel, seq=s),
        grid=(A_KV_HEADS, b // nb),
        in_specs=[
            pl.BlockSpec(memory_space=pltpu.SMEM),
            pl.BlockSpec((nb, s, gw), lambda g, bi: (bi, 0, g)),
            pl.BlockSpec((nb, s, HEAD_DIM), lambda g, bi: (bi, 0, A_K_COL + g)),
            pl.BlockSpec((nb, s, HEAD_DIM), lambda g, bi: (bi, 0, A_V_COL + g)),
            pl.BlockSpec((A_GROUP,) + tbl.shape[1:], lambda g, bi: (g, 0, 0, 0)),
        ],
        out_specs=pl.BlockSpec((nb, s, gw), lambda g, bi: (bi, 0, g)),
        out_shape=jax.ShapeDtypeStruct((b, s, A_Q_HEADS * HEAD_DIM), jnp.bfloat16),
        compiler_params=_cparams(("parallel", "parallel")),
        name="attn_a",
    )(sink, p3, p3, p3, tbl)


def _attn_b_kernel(*refs, seq):
    batch_rows = refs[0].shape[0]
    lax.fori_loop(0, batch_rows, functools.partial(_attn_b_one, refs=refs, seq=seq), 0)


def _attn_b_one(bb, carry, *, refs, seq):
    q_ref, k_ref, v_ref, tbl_ref, o_ref, qf, kf, vf, acc0, acc1, acc2, m0, m1, m2, l0, l1, l2 = refs
    qb_rows = 2 * B_HALF
    kw = 4 * B_HALF

    def attend(qb, kb, vb, tb):
        s = lax.dot_general(qb, kb, _NT, preferred_element_type=jnp.float32)
        t = s + tb
        m = jnp.max(t, axis=-1, keepdims=True)
        e = jnp.exp2(t - m).astype(jnp.bfloat16)
        al = jnp.dot(e, _with_ones(vb), preferred_element_type=jnp.float32)
        return al[:, :HEAD_DIM], jnp.broadcast_to(m, (qb.shape[0], HEAD_DIM)), al[:, HEAD_DIM:]

    nblk = seq // qb_rows

    def nat_block(i):
        qrow = pl.multiple_of(i * qb_rows, qb_rows)
        krow = pl.multiple_of(jnp.clip(i * qb_rows - B_HALF, 0, seq - kw), B_HALF)
        a, m, l = attend(q_ref[bb, pl.ds(qrow, qb_rows), :], k_ref[bb, pl.ds(krow, kw), :],
                         v_ref[bb, pl.ds(krow, kw), :], tbl_ref[0, _edge_variant(i, nblk)])
        acc0[pl.ds(qrow, qb_rows), :] = a
        m0[pl.ds(qrow, qb_rows), :] = m
        l0[pl.ds(qrow, qb_rows), :] = l

    def nat_body(it, carry):
        for u in range(B_UNROLL):
            nat_block(it * B_UNROLL + u)
        return carry

    lax.fori_loop(0, nblk // B_UNROLL, nat_body, 0)

    qf[...] = q_ref[bb].astype(jnp.float32)
    kf[...] = k_ref[bb].astype(jnp.float32)
    vf[...] = v_ref[bb].astype(jnp.float32)

    def dilated(d, var0, acc, mm, ll):
        ln = seq // d
        nb = ln // qb_rows
        per_iter = max(B_UNROLL // nb, 1)

        def residue(r):
            qr = qf[pl.ds(r, ln, stride=d), :].astype(jnp.bfloat16)
            kr = kf[pl.ds(r, ln, stride=d), :].astype(jnp.bfloat16)
            vr = vf[pl.ds(r, ln, stride=d), :].astype(jnp.bfloat16)
            for i in range(nb):
                if nb == 1:
                    k0, width, var = 0, ln, var0
                else:
                    k0 = min(max(i * qb_rows - B_HALF, 0), ln - kw)
                    width = kw
                    var = var0 + (0 if i == 0 else (2 if i == nb - 1 else 1))
                a, m, l = attend(qr[i * qb_rows:(i + 1) * qb_rows], kr[k0:k0 + width], vr[k0:k0 + width],
                                 tbl_ref[0, var, :, :width])
                rows = pl.ds(r + d * qb_rows * i, qb_rows, stride=d)
                acc[rows, :] = a
                mm[rows, :] = m
                ll[rows, :] = l

        def body(it, carry):
            for u in range(per_iter):
                residue(it * per_iter + u)
            return carry

        lax.fori_loop(0, d // per_iter, body, 0)

    dilated(B_CONFIGS[1][1], 3, acc1, m1, l1)
    dilated(B_CONFIGS[2][1], 6, acc2, m2, l2)

    merge_rows = 2 * qb_rows

    def merge(c, carry):
        rows = pl.ds(pl.multiple_of(c * merge_rows, merge_rows), merge_rows)
        ma, mb, mc = m0[rows, :], m1[rows, :], m2[rows, :]
        mx = jnp.maximum(jnp.maximum(ma, mb), mc)
        wa, wb, wc = jnp.exp2(ma - mx), jnp.exp2(mb - mx), jnp.exp2(mc - mx)
        num = wa * acc0[rows, :] + wb * acc1[rows, :] + wc * acc2[rows, :]
        den = wa * l0[rows, :] + wb * l1[rows, :] + wc * l2[rows, :]
        o_ref[bb, rows, :] = (num / den).astype(o_ref.dtype)
        return carry

    lax.fori_loop(0, seq // merge_rows, merge, 0)
    return carry


def _attn_b(p3, tbl):
    b, s, _ = p3.shape
    f32_slab = pltpu.VMEM((s, HEAD_DIM), jnp.float32)
    nb = math.gcd(b, ATTN_BATCH)
    return pl.pallas_call(
        functools.partial(_attn_b_kernel, seq=s),
        grid=(B_HEADS, b // nb),
        in_specs=[
            pl.BlockSpec((nb, s, HEAD_DIM), lambda h, bi: (bi, 0, B_Q_COL + h)),
            pl.BlockSpec((nb, s, HEAD_DIM), lambda h, bi: (bi, 0, B_K_COL + h)),
            pl.BlockSpec((nb, s, HEAD_DIM), lambda h, bi: (bi, 0, B_V_COL + h)),
            pl.BlockSpec((1,) + tbl.shape[1:], lambda h, bi: (h, 0, 0, 0)),
        ],
        out_specs=pl.BlockSpec((nb, s, HEAD_DIM), lambda h, bi: (bi, 0, h)),
        out_shape=jax.ShapeDtypeStruct((b, s, B_HEADS * HEAD_DIM), jnp.bfloat16),
        scratch_shapes=[f32_slab] * 12,
        compiler_params=_cparams(("parallel", "parallel")),
        name="attn_b",
    )(p3, p3, p3, tbl)


def _attn_c_kernel(q_ref, k_ref, v_ref, tbl_ref, o_ref, *, starts, variant_of):
    nq = NA_QROWS * GRID_W
    nk = NA_KROWS * GRID_W

    def one(bb, carry):
        for g, (ws, var) in enumerate(zip(starts, variant_of)):
            qb = q_ref[bb, g * nq:(g + 1) * nq, :]
            kb = k_ref[bb, ws * GRID_W:ws * GRID_W + nk, :]
            vb = v_ref[bb, ws * GRID_W:ws * GRID_W + nk, :]
            s = lax.dot_general(qb, kb, _NT, preferred_element_type=jnp.float32)
            t = s + tbl_ref[0, var]
            m = jnp.max(t, axis=-1, keepdims=True)
            e = jnp.exp2(t - m).astype(jnp.bfloat16)
            ol = jnp.dot(e, _with_ones(vb), preferred_element_type=jnp.float32)
            o_ref[bb, g * nq:(g + 1) * nq, :] = (ol[:, :HEAD_DIM] / ol[:, HEAD_DIM:]).astype(o_ref.dtype)
        return carry

    lax.fori_loop(0, q_ref.shape[0], one, 0)


def _attn_c(p3, tbl, starts, variant_of):
    b, s, _ = p3.shape
    nb = math.gcd(b, ATTN_BATCH)
    return pl.pallas_call(
        functools.partial(_attn_c_kernel, starts=starts, variant_of=variant_of),
        grid=(C_HEADS, b // nb),
        in_specs=[
            pl.BlockSpec((nb, s, HEAD_DIM), lambda h, bi: (bi, 0, h)),
            pl.BlockSpec((nb, s, HEAD_DIM), lambda h, bi: (bi, 0, C_HEADS + h)),
            pl.BlockSpec((nb, s, HEAD_DIM), lambda h, bi: (bi, 0, 2 * C_HEADS + h)),
            pl.BlockSpec((1,) + tbl.shape[1:], lambda h, bi: (h, 0, 0, 0)),
        ],
        out_specs=pl.BlockSpec((nb, s, HEAD_DIM), lambda h, bi: (bi, 0, h)),
        out_shape=jax.ShapeDtypeStruct((b, s, C_HEADS * HEAD_DIM), jnp.bfloat16),
        compiler_params=_cparams(("parallel", "parallel")),
        name="attn_c",
    )(p3, p3, p3, tbl)


def _gate_out_kernel(*refs, n_y, n_z, final_norm):
    y_refs = refs[:n_y]
    z_refs = refs[n_y:n_y + n_z]
    w_ref, x_ref = refs[n_y + n_z:n_y + n_z + 2]
    o_ref = refs[-1]
    y = jnp.concatenate([r[...] for r in y_refs], axis=1).astype(jnp.float32)
    z = jnp.concatenate([r[...] for r in z_refs], axis=1).astype(jnp.float32)
    gated = (y * (z / (1.0 + jnp.exp(-z)))).astype(jnp.bfloat16)
    out = x_ref[...] + jnp.dot(gated, w_ref[...], preferred_element_type=jnp.float32)
    if final_norm:
        g_ref = refs[-2]
        ms = jnp.mean(out * out, axis=-1, keepdims=True)
        out = out * lax.rsqrt(ms + RMS_EPS) * g_ref[...]
    o_ref[...] = out


def _gate_out(ys, p2d, z_col, z_blocks, w, x2d, g_final, *, tm, name):
    m, d = x2d.shape
    zw = d // z_blocks
    zb0 = z_col * HEAD_DIM // zw
    assert zb0 * zw == z_col * HEAD_DIM
    in_specs = [pl.BlockSpec((tm, y.shape[1]), lambda i: (i, 0)) for y in ys]
    in_specs += [pl.BlockSpec((tm, zw), functools.partial(lambda i, c: (i, c), c=zb0 + c)) for c in range(z_blocks)]
    in_specs += [pl.BlockSpec(w.shape, lambda i: (0, 0)), pl.BlockSpec((tm, d), lambda i: (i, 0))]
    args = list(ys) + [p2d] * z_blocks + [w, x2d]
    if g_final is not None:
        in_specs.append(pl.BlockSpec((1, d), lambda i: (0, 0)))
        args.append(g_final.reshape(1, d))
    return pl.pallas_call(
        functools.partial(_gate_out_kernel, n_y=len(ys), n_z=z_blocks, final_norm=g_final is not None),
        grid=(m // tm,),
        in_specs=in_specs,
        out_specs=pl.BlockSpec((tm, d), lambda i: (i, 0)),
        out_shape=jax.ShapeDtypeStruct((m, d), jnp.float32),
        compiler_params=_cparams(("parallel",)),
        name=name,
    )(*args)


def kernel(x, ln_ab, w_in_ab, sink_a, w_out_ab, ln_c, w_in_c, rpb_c, w_out_c, ln_f):
    b, s, d = x.shape
    assert s == 2048 and d == 2048, "tiling below is written for SEQ = D_MODEL = 2048"
    assert ln_ab.shape[0] == 1 and ln_c.shape[0] == 1, "depth-2 trunk: one layer of each kind"
    m = b * s
    bf16 = jnp.bfloat16
    x2d = x.reshape(m, d)

    slopes = _alibi_slopes(A_Q_HEADS + B_HEADS)
    aw = A_HALF_WINDOW
    tbl_a = jnp.stack([_band_bias(aw, 3 * aw, off, aw, 1, slopes[:A_Q_HEADS]) for off in (0, aw, 2 * aw)],
                      axis=1)
    qb, kw = 2 * B_HALF, 4 * B_HALF
    tbl_b = jnp.stack(
        [_band_bias(qb, kw, off, B_HALF, dil, slopes[A_Q_HEADS:]) for dil in (1, 4) for off in (0, B_HALF, 2 * B_HALF)]
        + [_band_bias(qb, kw, 0, B_HALF, B_CONFIGS[2][1], slopes[A_Q_HEADS:])], axis=1)
    na_starts, na_variant_of, na_variants = _na_groups(s // GRID_W)
    tbl_c = _na_bias_tables(rpb_c[0], na_variants)

    hd = HEAD_DIM
    q_cols_ab = ((A_Q_COL * hd, A_K_COL * hd), (B_Q_COL * hd, B_K_COL * hd))
    p = _norm_proj(x2d, ln_ab[0], w_in_ab[0].astype(bf16), q_cols_ab, tm=512, tn=3328, name="norm_proj_ab")
    p3 = p.reshape(b, s, p.shape[1])
    ya = _attn_a(p3, sink_a[0], tbl_a)
    yb = _attn_b(p3, tbl_b)
    x1 = _gate_out([ya.reshape(m, -1), yb.reshape(m, -1)], p, Z_AB_COL, 4, w_out_ab[0].astype(bf16), x2d, None,
                   tm=512, name="gate_out_ab")

    pc = _norm_proj(x1, ln_c[0], w_in_c[0].astype(bf16), ((0, C_HEADS * hd),), tm=1024, tn=2048, name="norm_proj_c")
    yc = _attn_c(pc.reshape(b, s, pc.shape[1]), tbl_c, na_starts, na_variant_of)
    out = _gate_out([yc.reshape(m, -1)], pc, 3 * C_HEADS, 1, w_out_c[0].astype(bf16), x1, ln_f,
                    tm=512, name="gate_out_c")
    return out.reshape(b, s, d)
```

```python
import functools
import math

import numpy as np
import jax
import jax.numpy as jnp
from jax import lax
from jax.experimental import pallas as pl
from jax.experimental.pallas import tpu as pltpu

HEAD_DIM = 128
A_Q_HEADS = 8
A_KV_HEADS = 2
A_GROUP = A_Q_HEADS // A_KV_HEADS
A_HALF_WINDOW = 128
B_HEADS = 8
B_CONFIGS = ((128, 1), (512, 4), (2048, 16))
B_HALF = 64
C_HEADS = 16
GRID_W = 64
NA_KH = 8
NA_KW = 16
RMS_EPS = 1e-5
NEG = -1e30
LOG2E = 1.4426950408889634
QK_SCALE = HEAD_DIM ** -0.5 * LOG2E

A_Q_COL = 0
A_K_COL = A_Q_HEADS
A_V_COL = A_K_COL + A_KV_HEADS
B_Q_COL = A_V_COL + A_KV_HEADS
B_K_COL = B_Q_COL + B_HEADS
B_V_COL = B_K_COL + B_HEADS
Z_AB_COL = B_V_COL + B_HEADS

NA_QROWS = 2
NA_KROWS = NA_QROWS + NA_KH - 1

A_UNROLL = 4
B_UNROLL = 16
ATTN_BATCH = 4

VMEM_LIMIT = 56 * 1024 * 1024

_NT = (((1,), (1,)), ((), ()))


def _cparams(sem):
    return pltpu.CompilerParams(dimension_semantics=sem, vmem_limit_bytes=VMEM_LIMIT)


def _edge_variant(i, n):
    return jnp.where(i == 0, 0, jnp.where(i == n - 1, 2, 1))


def _with_ones(vb):
    return jnp.concatenate([vb, jnp.ones_like(vb)], axis=1)


def _norm_proj_kernel(x_ref, g_ref, w_ref, cs_ref, o_ref, hn_ref):
    @pl.when(pl.program_id(1) == 0)
    def _():
        x = x_ref[...]
        ms = jnp.mean(x * x, axis=-1, keepdims=True)
        hn_ref[...] = (x * lax.rsqrt(ms + RMS_EPS) * g_ref[...]).astype(hn_ref.dtype)

    acc = jnp.dot(hn_ref[...], w_ref[...], preferred_element_type=jnp.float32)
    o_ref[...] = (acc * cs_ref[...]).astype(o_ref.dtype)


def _norm_proj(x2d, g, w, q_cols, *, tm, tn, name):
    m, d = x2d.shape
    n = w.shape[1]
    cs = np.ones((1, n), np.float32)
    for lo, hi in q_cols:
        cs[:, lo:hi] = QK_SCALE
    return pl.pallas_call(
        _norm_proj_kernel,
        grid=(m // tm, n // tn),
        in_specs=[
            pl.BlockSpec((tm, d), lambda i, j: (i, 0)),
            pl.BlockSpec((1, d), lambda i, j: (0, 0)),
            pl.BlockSpec((d, tn), lambda i, j: (0, j)),
            pl.BlockSpec((1, tn), lambda i, j: (0, j)),
        ],
        out_specs=pl.BlockSpec((tm, tn), lambda i, j: (i, j)),
        out_shape=jax.ShapeDtypeStruct((m, n), jnp.bfloat16),
        scratch_shapes=[pltpu.VMEM((tm, d), jnp.bfloat16)],
        compiler_params=_cparams(("parallel", "arbitrary")),
        name=name,
    )(x2d, g.reshape(1, d), w, jnp.asarray(cs))


def _alibi_slopes(n):
    return jnp.exp2(-8.0 * jnp.arange(1, n + 1, dtype=jnp.float32) / n)


def _band_bias(nq, nk, off, half, dist_scale, slopes):
    rel = np.arange(nk)[None, :] - np.arange(nq)[:, None] - off
    valid = np.abs(rel) <= half
    dist = jnp.asarray((np.abs(rel) * dist_scale).astype(np.float32))
    bias = -slopes[:, None, None] * dist
    return jnp.where(valid[None], bias * LOG2E, NEG)


def _na_plan(rows):
    masked = 2 * NA_KH - 1
    starts, slabs, pairs = [], [], []
    for g in range(rows // NA_QROWS):
        ws = int(np.clip(g * NA_QROWS - NA_KH // 2, 0, rows - NA_KROWS))
        per_row = []
        for rr in range(NA_QROWS):
            r = g * NA_QROWS + rr
            r0 = int(np.clip(r - NA_KH // 2, 0, rows - NA_KH))
            assert ws <= r0 and r0 + NA_KH <= ws + NA_KROWS
            ids = [kr - r + NA_KH - 1 if r0 <= kr < r0 + NA_KH else masked for kr in range(ws, ws + NA_KROWS)]
            ids += [masked] * (len(ids) % 2)
            row_pairs = []
            for p in range(0, len(ids), 2):
                pair = (ids[p], ids[p + 1])
                if pair not in pairs:
                    pairs.append(pair)
                row_pairs.append(pairs.index(pair))
            per_row.append(tuple(row_pairs))
        starts.append(ws)
        slabs.append(tuple(per_row))
    return tuple(starts), tuple(slabs), tuple(pairs)


def _na_bias_bank(rpb, pairs):
    h, n_dr, n_dc = rpb.shape
    cq = np.arange(GRID_W)
    c0 = np.clip(cq - NA_KW // 2, 0, GRID_W - NA_KW)
    ck = np.arange(GRID_W)
    col_valid = (ck[None, :] >= c0[:, None]) & (ck[None, :] < c0[:, None] + NA_KW)
    dc = ck[None, :] - cq[:, None] + NA_KW - 1
    onehot = (dc[None] == np.arange(n_dc)[:, None, None]) & col_valid[None]
    rhs = np.zeros((2, n_dc, GRID_W, 2, GRID_W), np.float32)
    for j in range(2):
        rhs[j, :, :, j, :] = onehot
    sel = np.asarray(pairs, np.int32)
    rp = jnp.concatenate([rpb * LOG2E, jnp.zeros((h, 1, n_dc), jnp.float32)], axis=1)
    lhs = jnp.take(rp, jnp.asarray(sel.reshape(-1)), axis=1).reshape(h, len(pairs), 2 * n_dc)
    bank = jnp.einsum("hnx,xqk->hnqk", lhs, jnp.asarray(rhs.reshape(2 * n_dc, GRID_W, 2 * GRID_W)),
                      precision=lax.Precision.HIGHEST)
    valid = (sel != n_dr)[:, None, :, None] & col_valid[None, :, None, :]
    return jnp.where(jnp.asarray(valid.reshape(len(pairs), GRID_W, 2 * GRID_W))[None], bank, NEG)


def _attn_a_kernel(sink_ref, q_ref, k_ref, v_ref, tbl_ref, o_ref, *, seq):
    blk = A_HALF_WINDOW
    nk = 3 * blk
    nblk = seq // blk
    g = pl.program_id(0)
    sinks = [sink_ref[g * A_GROUP + j] * LOG2E for j in range(A_GROUP)]

    def block(bb, i):
        qrow = pl.multiple_of(i * blk, blk)
        krow = pl.multiple_of(jnp.clip((i - 1) * blk, 0, seq - nk), blk)
        var = _edge_variant(i, nblk)
        q4 = q_ref[bb, pl.ds(qrow, blk), :]
        qs = jnp.concatenate([q4[:, j * HEAD_DIM:(j + 1) * HEAD_DIM] for j in range(A_GROUP)], axis=0)
        kb = k_ref[bb, pl.ds(krow, nk), :]
        vb = v_ref[bb, pl.ds(krow, nk), :]
        s = lax.dot_general(qs, kb, _NT, preferred_element_type=jnp.float32)
        es, ls = [], []
        for j in range(A_GROUP):
            t = s[j * blk:(j + 1) * blk, :] + tbl_ref[j, var]
            m = jnp.maximum(jnp.max(t, axis=-1, keepdims=True), sinks[j])
            e = jnp.exp2(t - m)
            ls.append(jnp.sum(e, axis=-1, keepdims=True) + jnp.exp2(sinks[j] - m))
            es.append(e.astype(jnp.bfloat16))
        o = jnp.dot(jnp.concatenate(es, axis=0), vb, preferred_element_type=jnp.float32)
        out = jnp.concatenate([o[j * blk:(j + 1) * blk] / ls[j] for j in range(A_GROUP)], axis=1)
        o_ref[bb, pl.ds(qrow, blk), :] = out.astype(o_ref.dtype)

    iters = nblk // A_UNROLL

    def body(n, carry):
        bb, it = n // iters, n % iters
        for u in range(A_UNROLL):
            block(bb, it * A_UNROLL + u)
        return carry

    lax.fori_loop(0, q_ref.shape[0] * iters, body, 0)


def _attn_a(p3, sink, tbl):
    b, s, _ = p3.shape
    gw = A_GROUP * HEAD_DIM
    nb = math.gcd(b, ATTN_BATCH)
    return pl.pallas_call(
        functools.partial(_attn_a_kernel, seq=s),
        grid=(A_KV_HEADS, b // nb),
        in_specs=[
            pl.BlockSpec(memory_space=pltpu.SMEM),
            pl.BlockSpec((nb, s, gw), lambda g, bi: (bi, 0, g)),
            pl.BlockSpec((nb, s, HEAD_DIM), lambda g, bi: (bi, 0, A_K_COL + g)),
            pl.BlockSpec((nb, s, HEAD_DIM), lambda g, bi: (bi, 0, A_V_COL + g)),
            pl.BlockSpec((A_GROUP,) + tbl.shape[1:], lambda g, bi: (g, 0, 0, 0)),
        ],
        out_specs=pl.BlockSpec((nb, s, gw), lambda g, bi: (bi, 0, g)),
        out_shape=jax.ShapeDtypeStruct((b, s, A_Q_HEADS * HEAD_DIM), jnp.bfloat16),
        compiler_params=_cparams(("parallel", "parallel")),
        name="attn_a",
    )(sink, p3, p3, p3, tbl)


def _attn_b_kernel(*refs, seq):
    batch_rows = refs[0].shape[0]
    lax.fori_loop(0, batch_rows, functools.partial(_attn_b_one, refs=refs, seq=seq), 0)


def _attn_b_one(bb, carry, *, refs, seq):
    (q_ref, k_ref, v_ref, tbl_ref, o_ref, qf, kf, vf, q4, k4, v4, of,
     acc0, acc1, acc2, m0, m1, m2, l0, l1, l2) = refs
    qb_rows = 2 * B_HALF
    kw = 4 * B_HALF
    fold = B_CONFIGS[1][1]

    def attend(qb, kb, vb, tb):
        s = lax.dot_general(qb, kb, _NT, preferred_element_type=jnp.float32)
        t = s + tb
        m = jnp.max(t, axis=-1, keepdims=True)
        e = jnp.exp2(t - m).astype(jnp.bfloat16)
        al = jnp.dot(e, _with_ones(vb), preferred_element_type=jnp.float32)
        return al[:, :HEAD_DIM], jnp.broadcast_to(m, (qb.shape[0], HEAD_DIM)), al[:, HEAD_DIM:]

    nblk = seq // qb_rows

    def nat_block(i):
        qrow = pl.multiple_of(i * qb_rows, qb_rows)
        krow = pl.multiple_of(jnp.clip(i * qb_rows - B_HALF, 0, seq - kw), B_HALF)
        a, m, l = attend(q_ref[bb, pl.ds(qrow, qb_rows), :], k_ref[bb, pl.ds(krow, kw), :],
                         v_ref[bb, pl.ds(krow, kw), :], tbl_ref[0, _edge_variant(i, nblk)])
        acc0[pl.ds(qrow, qb_rows), :] = a
        m0[pl.ds(qrow, qb_rows), :] = m
        l0[pl.ds(qrow, qb_rows), :] = l

    def nat_body(it, carry):
        for u in range(B_UNROLL):
            nat_block(it * B_UNROLL + u)
        return carry

    lax.fori_loop(0, nblk // B_UNROLL, nat_body, 0)

    qf[...] = q_ref[bb].astype(jnp.float32)
    kf[...] = k_ref[bb].astype(jnp.float32)
    vf[...] = v_ref[bb].astype(jnp.float32)

    def dilated(d, var0, acc, mm, ll):
        ln = seq // d
        nb = ln // qb_rows
        per_iter = max(B_UNROLL // nb, 1)

        def residue(r):
            if d == fold:
                cls = [src[pl.ds(r, ln, stride=d), :] for src in (qf, kf, vf)]
                for dst, val in zip((q4, k4, v4), cls):
                    dst[pl.ds(pl.multiple_of(r * ln, ln), ln), :] = val
            else:
                first = (r % fold) * (seq // fold) + r // fold
                cls = [src[pl.ds(first, ln, stride=d // fold), :] for src in (q4, k4, v4)]
            qr, kr, vr = (val.astype(jnp.bfloat16) for val in cls)
            for i in range(nb):
                if nb == 1:
                    k0, width, var = 0, ln, var0
                else:
                    k0 = min(max(i * qb_rows - B_HALF, 0), ln - kw)
                    width = kw
                    var = var0 + (0 if i == 0 else (2 if i == nb - 1 else 1))
                a, m, l = attend(qr[i * qb_rows:(i + 1) * qb_rows], kr[k0:k0 + width], vr[k0:k0 + width],
                                 tbl_ref[0, var, :, :width])
                start = (r % fold) * (seq // fold) + r // fold + (d // fold) * qb_rows * i
                if d == fold:
                    rows = pl.ds(pl.multiple_of(start, qb_rows), qb_rows)
                else:
                    rows = pl.ds(start, qb_rows, stride=d // fold)
                acc[rows, :] = a
                mm[rows, :] = m
                ll[rows, :] = l

        def body(it, carry):
            for u in range(per_iter):
                residue(it * per_iter + u)
            return carry

        lax.fori_loop(0, d // per_iter, body, 0)

    dilated(B_CONFIGS[1][1], 3, acc1, m1, l1)
    dilated(B_CONFIGS[2][1], 6, acc2, m2, l2)

    merge_rows = 2 * qb_rows
    chunks_per_class = seq // fold // merge_rows

    def merge(c, carry):
        rows = pl.ds(pl.multiple_of(c * merge_rows, merge_rows), merge_rows)
        nat = pl.ds(c // chunks_per_class + fold * merge_rows * (c % chunks_per_class), merge_rows, stride=fold)
        ma, mb, mc = m0[nat, :], m1[rows, :], m2[rows, :]
        mx = jnp.maximum(jnp.maximum(ma, mb), mc)
        wa, wb, wc = jnp.exp2(ma - mx), jnp.exp2(mb - mx), jnp.exp2(mc - mx)
        num = wa * acc0[nat, :] + wb * acc1[rows, :] + wc * acc2[rows, :]
        den = wa * l0[nat, :] + wb * l1[rows, :] + wc * l2[rows, :]
        of[nat, :] = num / den
        return carry

    lax.fori_loop(0, seq // merge_rows, merge, 0)
    o_ref[bb] = of[...].astype(o_ref.dtype)
    return carry


def _attn_b(p3, tbl):
    b, s, _ = p3.shape
    f32_slab = pltpu.VMEM((s, HEAD_DIM), jnp.float32)
    nb = math.gcd(b, ATTN_BATCH)
    return pl.pallas_call(
        functools.partial(_attn_b_kernel, seq=s),
        grid=(B_HEADS, b // nb),
        in_specs=[
            pl.BlockSpec((nb, s, HEAD_DIM), lambda h, bi: (bi, 0, B_Q_COL + h)),
            pl.BlockSpec((nb, s, HEAD_DIM), lambda h, bi: (bi, 0, B_K_COL + h)),
            pl.BlockSpec((nb, s, HEAD_DIM), lambda h, bi: (bi, 0, B_V_COL + h)),
            pl.BlockSpec((1,) + tbl.shape[1:], lambda h, bi: (h, 0, 0, 0)),
        ],
        out_specs=pl.BlockSpec((nb, s, HEAD_DIM), lambda h, bi: (bi, 0, h)),
        out_shape=jax.ShapeDtypeStruct((b, s, B_HEADS * HEAD_DIM), jnp.bfloat16),
        scratch_shapes=[f32_slab] * 16,
        compiler_params=_cparams(("parallel", "parallel")),
        name="attn_b",
    )(p3, p3, p3, tbl)


def _attn_c_kernel(q_ref, k_ref, v_ref, bank_ref, o_ref, *, starts, slabs):
    nq = NA_QROWS * GRID_W
    nk = NA_KROWS * GRID_W
    pw = 2 * GRID_W

    def one(bb, carry):
        for g, (ws, per_row) in enumerate(zip(starts, slabs)):
            qb = q_ref[bb, g * nq:(g + 1) * nq, :]
            kb = k_ref[bb, ws * GRID_W:ws * GRID_W + nk, :]
            vb = v_ref[bb, ws * GRID_W:ws * GRID_W + nk, :]
            s = lax.dot_general(qb, kb, _NT, preferred_element_type=jnp.float32)
            t_rows = []
            for rr, row_pairs in enumerate(per_row):
                pieces = []
                for p, slab in enumerate(row_pairs):
                    width = min(pw, nk - p * pw)
                    pieces.append(s[rr * GRID_W:(rr + 1) * GRID_W, p * pw:p * pw + width]
                                  + bank_ref[0, slab, :, :width])
                t_rows.append(jnp.concatenate(pieces, axis=1))
            t = jnp.concatenate(t_rows, axis=0)
            m = jnp.max(t, axis=-1, keepdims=True)
            e = jnp.exp2(t - m).astype(jnp.bfloat16)
            ol = jnp.dot(e, _with_ones(vb), preferred_element_type=jnp.float32)
            o_ref[bb, g * nq:(g + 1) * nq, :] = (ol[:, :HEAD_DIM] / ol[:, HEAD_DIM:]).astype(o_ref.dtype)
        return carry

    lax.fori_loop(0, q_ref.shape[0], one, 0)


def _attn_c(p3, tbl, starts, slabs):
    b, s, _ = p3.shape
    nb = math.gcd(b, ATTN_BATCH)
    return pl.pallas_call(
        functools.partial(_attn_c_kernel, starts=starts, slabs=slabs),
        grid=(C_HEADS, b // nb),
        in_specs=[
            pl.BlockSpec((nb, s, HEAD_DIM), lambda h, bi: (bi, 0, h)),
            pl.BlockSpec((nb, s, HEAD_DIM), lambda h, bi: (bi, 0, C_HEADS + h)),
            pl.BlockSpec((nb, s, HEAD_DIM), lambda h, bi: (bi, 0, 2 * C_HEADS + h)),
            pl.BlockSpec((1,) + tbl.shape[1:], lambda h, bi: (h, 0, 0, 0)),
        ],
        out_specs=pl.BlockSpec((nb, s, HEAD_DIM), lambda h, bi: (bi, 0, h)),
        out_shape=jax.ShapeDtypeStruct((b, s, C_HEADS * HEAD_DIM), jnp.bfloat16),
        compiler_params=_cparams(("parallel", "parallel")),
        name="attn_c",
    )(p3, p3, p3, tbl)


def _gate_out_kernel(*refs, n_y, n_z, final_norm):
    y_refs = refs[:n_y]
    z_refs = refs[n_y:n_y + n_z]
    w_ref, x_ref = refs[n_y + n_z:n_y + n_z + 2]
    o_ref = refs[-1]
    y = jnp.concatenate([r[...] for r in y_refs], axis=1).astype(jnp.float32)
    z = jnp.concatenate([r[...] for r in z_refs], axis=1).astype(jnp.float32)
    gated = (y * (z / (1.0 + jnp.exp(-z)))).astype(jnp.bfloat16)
    out = x_ref[...] + jnp.dot(gated, w_ref[...], preferred_element_type=jnp.float32)
    if final_norm:
        g_ref = refs[-2]
        ms = jnp.mean(out * out, axis=-1, keepdims=True)
        out = out * lax.rsqrt(ms + RMS_EPS) * g_ref[...]
    o_ref[...] = out


def _gate_out(ys, p2d, z_col, z_blocks, w, x2d, g_final, *, tm, name):
    m, d = x2d.shape
    zw = d // z_blocks
    zb0 = z_col * HEAD_DIM // zw
    assert zb0 * zw == z_col * HEAD_DIM
    in_specs = [pl.BlockSpec((tm, y.shape[1]), lambda i: (i, 0)) for y in ys]
    in_specs += [pl.BlockSpec((tm, zw), functools.partial(lambda i, c: (i, c), c=zb0 + c)) for c in range(z_blocks)]
    in_specs += [pl.BlockSpec(w.shape, lambda i: (0, 0)), pl.BlockSpec((tm, d), lambda i: (i, 0))]
    args = list(ys) + [p2d] * z_blocks + [w, x2d]
    if g_final is not None:
        in_specs.append(pl.BlockSpec((1, d), lambda i: (0, 0)))
        args.append(g_final.reshape(1, d))
    return pl.pallas_call(
        functools.partial(_gate_out_kernel, n_y=len(ys), n_z=z_blocks, final_norm=g_final is not None),
        grid=(m // tm,),
        in_specs=in_specs,
        out_specs=pl.BlockSpec((tm, d), lambda i: (i, 0)),
        out_shape=jax.ShapeDtypeStruct((m, d), jnp.float32),
        compiler_params=_cparams(("parallel",)),
        name=name,
    )(*args)


def kernel(x, ln_ab, w_in_ab, sink_a, w_out_ab, ln_c, w_in_c, rpb_c, w_out_c, ln_f):
    b, s, d = x.shape
    assert s == 2048 and d == 2048, "tiling below is written for SEQ = D_MODEL = 2048"
    assert ln_ab.shape[0] == 1 and ln_c.shape[0] == 1, "depth-2 trunk: one layer of each kind"
    m = b * s
    bf16 = jnp.bfloat16
    x2d = x.reshape(m, d)

    slopes = _alibi_slopes(A_Q_HEADS + B_HEADS)
    aw = A_HALF_WINDOW
    tbl_a = jnp.stack([_band_bias(aw, 3 * aw, off, aw, 1, slopes[:A_Q_HEADS]) for off in (0, aw, 2 * aw)],
                      axis=1)
    qb, kw = 2 * B_HALF, 4 * B_HALF
    tbl_b = jnp.stack(
        [_band_bias(qb, kw, off, B_HALF, dil, slopes[A_Q_HEADS:]) for dil in (1, 4) for off in (0, B_HALF, 2 * B_HALF)]
        + [_band_bias(qb, kw, 0, B_HALF, B_CONFIGS[2][1], slopes[A_Q_HEADS:])], axis=1)
    na_starts, na_slabs, na_pairs = _na_plan(s // GRID_W)
    tbl_c = _na_bias_bank(rpb_c[0], na_pairs)

    hd = HEAD_DIM
    q_cols_ab = ((A_Q_COL * hd, A_K_COL * hd), (B_Q_COL * hd, B_K_COL * hd))
    p = _norm_proj(x2d, ln_ab[0], w_in_ab[0].astype(bf16), q_cols_ab, tm=512, tn=3328, name="norm_proj_ab")
    p3 = p.reshape(b, s, p.shape[1])
    ya = _attn_a(p3, sink_a[0], tbl_a)
    yb = _attn_b(p3, tbl_b)
    x1 = _gate_out([ya.reshape(m, -1), yb.reshape(m, -1)], p, Z_AB_COL, 4, w_out_ab[0].astype(bf16), x2d, None,
                   tm=512, name="gate_out_ab")

    pc = _norm_proj(x1, ln_c[0], w_in_c[0].astype(bf16), ((0, C_HEADS * hd),), tm=1024, tn=2048, name="norm_proj_c")
    yc = _attn_c(pc.reshape(b, s, pc.shape[1]), tbl_c, na_starts, na_slabs)
    out = _gate_out([yc.reshape(m, -1)], pc, 3 * C_HEADS, 1, w_out_c[0].astype(bf16), x1, ln_f,
                    tm=512, name="gate_out_c")
    return out.reshape(b, s, d)
```

```python
import functools
import math

import numpy as np
import jax
import jax.numpy as jnp
from jax import lax
from jax.experimental import pallas as pl
from jax.experimental.pallas import tpu as pltpu

HEAD_DIM = 128
A_Q_HEADS = 8
A_KV_HEADS = 2
A_GROUP = A_Q_HEADS // A_KV_HEADS
A_HALF_WINDOW = 128
B_HEADS = 8
B_CONFIGS = ((128, 1), (512, 4), (2048, 16))
B_HALF = 64
C_HEADS = 16
GRID_W = 64
NA_KH = 8
NA_KW = 16
RMS_EPS = 1e-5
NEG = -1e30
LOG2E = 1.4426950408889634
QK_SCALE = HEAD_DIM ** -0.5 * LOG2E

A_Q_COL = 0
A_K_COL = A_Q_HEADS
A_V_COL = A_K_COL + A_KV_HEADS
B_Q_COL = A_V_COL + A_KV_HEADS
B_K_COL = B_Q_COL + B_HEADS
B_V_COL = B_K_COL + B_HEADS
Z_AB_COL = B_V_COL + B_HEADS

NA_QROWS = 2
NA_KROWS = NA_QROWS + NA_KH - 1

A_UNROLL = 8
B_UNROLL = 16
ATTN_BATCH = 4

VMEM_LIMIT = 56 * 1024 * 1024

_NT = (((1,), (1,)), ((), ()))


def _cparams(sem):
    return pltpu.CompilerParams(dimension_semantics=sem, vmem_limit_bytes=VMEM_LIMIT)


def _edge_variant(i, n):
    return jnp.where(i == 0, 0, jnp.where(i == n - 1, 2, 1))


def _with_ones(vb):
    return jnp.concatenate([vb, jnp.ones_like(vb)], axis=1)


def _norm_proj_kernel(x_ref, g_ref, w_ref, cs_ref, o_ref, hn_ref):
    @pl.when(pl.program_id(1) == 0)
    def _():
        x = x_ref[...]
        ms = jnp.mean(x * x, axis=-1, keepdims=True)
        hn_ref[...] = (x * lax.rsqrt(ms + RMS_EPS) * g_ref[...]).astype(hn_ref.dtype)

    acc = jnp.dot(hn_ref[...], w_ref[...], preferred_element_type=jnp.float32)
    o_ref[...] = (acc * cs_ref[...]).astype(o_ref.dtype)


def _norm_proj(x2d, g, w, q_cols, *, tm, tn, name):
    m, d = x2d.shape
    n = w.shape[1]
    cs = np.ones((1, n), np.float32)
    for lo, hi in q_cols:
        cs[:, lo:hi] = QK_SCALE
    return pl.pallas_call(
        _norm_proj_kernel,
        grid=(m // tm, n // tn),
        in_specs=[
            pl.BlockSpec((tm, d), lambda i, j: (i, 0)),
            pl.BlockSpec((1, d), lambda i, j: (0, 0)),
            pl.BlockSpec((d, tn), lambda i, j: (0, j)),
            pl.BlockSpec((1, tn), lambda i, j: (0, j)),
        ],
        out_specs=pl.BlockSpec((tm, tn), lambda i, j: (i, j)),
        out_shape=jax.ShapeDtypeStruct((m, n), jnp.bfloat16),
        scratch_shapes=[pltpu.VMEM((tm, d), jnp.bfloat16)],
        compiler_params=_cparams(("parallel", "arbitrary")),
        name=name,
    )(x2d, g.reshape(1, d), w, jnp.asarray(cs))


def _alibi_slopes(n):
    return jnp.exp2(-8.0 * jnp.arange(1, n + 1, dtype=jnp.float32) / n)


def _band_bias(nq, nk, off, half, dist_scale, slopes):
    rel = np.arange(nk)[None, :] - np.arange(nq)[:, None] - off
    valid = np.abs(rel) <= half
    dist = jnp.asarray((np.abs(rel) * dist_scale).astype(np.float32))
    bias = -slopes[:, None, None] * dist
    return jnp.where(valid[None], bias * LOG2E, NEG)


def _na_plan(rows):
    masked = 2 * NA_KH - 1
    starts, slabs, pairs = [], [], []
    for g in range(rows // NA_QROWS):
        ws = int(np.clip(g * NA_QROWS - NA_KH // 2, 0, rows - NA_KROWS))
        per_row = []
        for rr in range(NA_QROWS):
            r = g * NA_QROWS + rr
            r0 = int(np.clip(r - NA_KH // 2, 0, rows - NA_KH))
            assert ws <= r0 and r0 + NA_KH <= ws + NA_KROWS
            ids = [kr - r + NA_KH - 1 if r0 <= kr < r0 + NA_KH else masked for kr in range(ws, ws + NA_KROWS)]
            ids += [masked] * (len(ids) % 2)
            row_pairs = []
            for p in range(0, len(ids), 2):
                pair = (ids[p], ids[p + 1])
                if pair not in pairs:
                    pairs.append(pair)
                row_pairs.append(pairs.index(pair))
            per_row.append(tuple(row_pairs))
        starts.append(ws)
        slabs.append(tuple(per_row))
    return tuple(starts), tuple(slabs), tuple(pairs)


def _na_bias_bank(rpb, pairs):
    h, n_dr, n_dc = rpb.shape
    cq = np.arange(GRID_W)
    c0 = np.clip(cq - NA_KW // 2, 0, GRID_W - NA_KW)
    ck = np.arange(GRID_W)
    col_valid = (ck[None, :] >= c0[:, None]) & (ck[None, :] < c0[:, None] + NA_KW)
    dc = ck[None, :] - cq[:, None] + NA_KW - 1
    onehot = (dc[None] == np.arange(n_dc)[:, None, None]) & col_valid[None]
    rhs = np.zeros((2, n_dc, GRID_W, 2, GRID_W), np.float32)
    for j in range(2):
        rhs[j, :, :, j, :] = onehot
    sel = np.asarray(pairs, np.int32)
    rp = jnp.concatenate([rpb * LOG2E, jnp.zeros((h, 1, n_dc), jnp.float32)], axis=1)
    lhs = jnp.take(rp, jnp.asarray(sel.reshape(-1)), axis=1).reshape(h, len(pairs), 2 * n_dc)
    bank = jnp.einsum("hnx,xqk->hnqk", lhs, jnp.asarray(rhs.reshape(2 * n_dc, GRID_W, 2 * GRID_W)),
                      precision=lax.Precision.HIGHEST)
    valid = (sel != n_dr)[:, None, :, None] & col_valid[None, :, None, :]
    return jnp.where(jnp.asarray(valid.reshape(len(pairs), GRID_W, 2 * GRID_W))[None], bank, NEG)


def _attn_a_kernel(sink_ref, q_ref, k_ref, v_ref, tbl_ref, o_ref, *, seq):
    blk = A_HALF_WINDOW
    nk = 3 * blk
    nblk = seq // blk
    g = pl.program_id(0)
    sinks = [sink_ref[g * A_GROUP + j] * LOG2E for j in range(A_GROUP)]

    def block(bb, i):
        qrow = pl.multiple_of(i * blk, blk)
        krow = pl.multiple_of(jnp.clip((i - 1) * blk, 0, seq - nk), blk)
        var = _edge_variant(i, nblk)
        q4 = q_ref[bb, pl.ds(qrow, blk), :]
        qs = jnp.concatenate([q4[:, j * HEAD_DIM:(j + 1) * HEAD_DIM] for j in range(A_GROUP)], axis=0)
        kb = k_ref[bb, pl.ds(krow, nk), :]
        vb = v_ref[bb, pl.ds(krow, nk), :]
        s = lax.dot_general(qs, kb, _NT, preferred_element_type=jnp.float32)
        es, ls = [], []
        for j in range(A_GROUP):
            t = s[j * blk:(j + 1) * blk, :] + tbl_ref[j, var]
            m = jnp.maximum(jnp.max(t, axis=-1, keepdims=True), sinks[j])
            e = jnp.exp2(t - m)
            ls.append(jnp.sum(e, axis=-1, keepdims=True) + jnp.exp2(sinks[j] - m))
            es.append(e.astype(jnp.bfloat16))
        o = jnp.dot(jnp.concatenate(es, axis=0), vb, preferred_element_type=jnp.float32)
        out = jnp.concatenate([o[j * blk:(j + 1) * blk] / ls[j] for j in range(A_GROUP)], axis=1)
        o_ref[bb, pl.ds(qrow, blk), :] = out.astype(o_ref.dtype)

    iters = nblk // A_UNROLL

    def body(n, carry):
        bb, it = n // iters, n % iters
        for u in range(A_UNROLL):
            block(bb, it * A_UNROLL + u)
        return carry

    lax.fori_loop(0, q_ref.shape[0] * iters, body, 0)


def _attn_a(p3, sink, tbl):
    b, s, _ = p3.shape
    gw = A_GROUP * HEAD_DIM
    nb = math.gcd(b, ATTN_BATCH)
    return pl.pallas_call(
        functools.partial(_attn_a_kernel, seq=s),
        grid=(A_KV_HEADS, b // nb),
        in_specs=[
            pl.BlockSpec(memory_space=pltpu.SMEM),
            pl.BlockSpec((nb, s, gw), lambda g, bi: (bi, 0, g)),
            pl.BlockSpec((nb, s, HEAD_DIM), lambda g, bi: (bi, 0, A_K_COL + g)),
            pl.BlockSpec((nb, s, HEAD_DIM), lambda g, bi: (bi, 0, A_V_COL + g)),
            pl.BlockSpec((A_GROUP,) + tbl.shape[1:], lambda g, bi: (g, 0, 0, 0)),
        ],
        out_specs=pl.BlockSpec((nb, s, gw), lambda g, bi: (bi, 0, g)),
        out_shape=jax.ShapeDtypeStruct((b, s, A_Q_HEADS * HEAD_DIM), jnp.bfloat16),
        compiler_params=_cparams(("parallel", "parallel")),
        name="attn_a",
    )(sink, p3, p3, p3, tbl)


def _attn_b_kernel(*refs, seq):
    batch_rows = refs[0].shape[0]
    lax.fori_loop(0, batch_rows, functools.partial(_attn_b_one, refs=refs, seq=seq), 0)


def _attn_b_one(bb, carry, *, refs, seq):
    (q_ref, k_ref, v_ref, tbl_ref, o_ref, qf, kf, vf, q4, k4, v4, of,
     acc0, acc1, acc2, m0, m1, m2, l0, l1, l2) = refs
    qb_rows = 2 * B_HALF
    kw = 4 * B_HALF
    fold = B_CONFIGS[1][1]

    def attend(qb, kb, vb, tb):
        s = lax.dot_general(qb, kb, _NT, preferred_element_type=jnp.float32)
        t = s + tb
        m = jnp.max(t, axis=-1, keepdims=True)
        e = jnp.exp2(t - m).astype(jnp.bfloat16)
        al = jnp.dot(e, _with_ones(vb), preferred_element_type=jnp.float32)
        return al[:, :HEAD_DIM], jnp.broadcast_to(m, (qb.shape[0], HEAD_DIM)), al[:, HEAD_DIM:]

    nblk = seq // qb_rows

    def nat_block(i):
        qrow = pl.multiple_of(i * qb_rows, qb_rows)
        krow = pl.multiple_of(jnp.clip(i * qb_rows - B_HALF, 0, seq - kw), B_HALF)
        a, m, l = attend(q_ref[bb, pl.ds(qrow, qb_rows), :], k_ref[bb, pl.ds(krow, kw), :],
                         v_ref[bb, pl.ds(krow, kw), :], tbl_ref[0, _edge_variant(i, nblk)])
        acc0[pl.ds(qrow, qb_rows), :] = a
        m0[pl.ds(qrow, qb_rows), :] = m
        l0[pl.ds(qrow, qb_rows), :] = l

    def nat_body(it, carry):
        for u in range(B_UNROLL):
            nat_block(it * B_UNROLL + u)
        return carry

    lax.fori_loop(0, nblk // B_UNROLL, nat_body, 0)

    qf[...] = q_ref[bb].astype(jnp.float32)
    kf[...] = k_ref[bb].astype(jnp.float32)
    vf[...] = v_ref[bb].astype(jnp.float32)

    def dilated(d, var0, acc, mm, ll):
        ln = seq // d
        nb = ln // qb_rows
        per_iter = max(B_UNROLL // nb, 1)

        def residue(r):
            if d == fold:
                cls = [src[pl.ds(r, ln, stride=d), :] for src in (qf, kf, vf)]
                for dst, val in zip((q4, k4, v4), cls):
                    dst[pl.ds(pl.multiple_of(r * ln, ln), ln), :] = val
            else:
                first = (r % fold) * (seq // fold) + r // fold
                cls = [src[pl.ds(first, ln, stride=d // fold), :] for src in (q4, k4, v4)]
            qr, kr, vr = (val.astype(jnp.bfloat16) for val in cls)
            for i in range(nb):
                if nb == 1:
                    k0, width, var = 0, ln, var0
                else:
                    k0 = min(max(i * qb_rows - B_HALF, 0), ln - kw)
                    width = kw
                    var = var0 + (0 if i == 0 else (2 if i == nb - 1 else 1))
                a, m, l = attend(qr[i * qb_rows:(i + 1) * qb_rows], kr[k0:k0 + width], vr[k0:k0 + width],
                                 tbl_ref[0, var, :, :width])
                start = (r % fold) * (seq // fold) + r // fold + (d // fold) * qb_rows * i
                if d == fold:
                    rows = pl.ds(pl.multiple_of(start, qb_rows), qb_rows)
                else:
                    rows = pl.ds(start, qb_rows, stride=d // fold)
                acc[rows, :] = a
                mm[rows, :] = m
                ll[rows, :] = l

        def body(it, carry):
            for u in range(per_iter):
                residue(it * per_iter + u)
            return carry

        lax.fori_loop(0, d // per_iter, body, 0)

    dilated(B_CONFIGS[1][1], 3, acc1, m1, l1)
    dilated(B_CONFIGS[2][1], 6, acc2, m2, l2)

    merge_rows = 2 * qb_rows
    chunks_per_class = seq // fold // merge_rows

    def merge(c, carry):
        rows = pl.ds(pl.multiple_of(c * merge_rows, merge_rows), merge_rows)
        nat = pl.ds(c // chunks_per_class + fold * merge_rows * (c % chunks_per_class), merge_rows, stride=fold)
        ma, mb, mc = m0[nat, :], m1[rows, :], m2[rows, :]
        mx = jnp.maximum(jnp.maximum(ma, mb), mc)
        wa, wb, wc = jnp.exp2(ma - mx), jnp.exp2(mb - mx), jnp.exp2(mc - mx)
        num = wa * acc0[nat, :] + wb * acc1[rows, :] + wc * acc2[rows, :]
        den = wa * l0[nat, :] + wb * l1[rows, :] + wc * l2[rows, :]
        of[nat, :] = num / den
        return carry

    lax.fori_loop(0, seq // merge_rows, merge, 0)
    o_ref[bb] = of[...].astype(o_ref.dtype)
    return carry


def _attn_b(p3, tbl):
    b, s, _ = p3.shape
    f32_slab = pltpu.VMEM((s, HEAD_DIM), jnp.float32)
    nb = math.gcd(b, ATTN_BATCH)
    return pl.pallas_call(
        functools.partial(_attn_b_kernel, seq=s),
        grid=(B_HEADS, b // nb),
        in_specs=[
            pl.BlockSpec((nb, s, HEAD_DIM), lambda h, bi: (bi, 0, B_Q_COL + h)),
            pl.BlockSpec((nb, s, HEAD_DIM), lambda h, bi: (bi, 0, B_K_COL + h)),
            pl.BlockSpec((nb, s, HEAD_DIM), lambda h, bi: (bi, 0, B_V_COL + h)),
            pl.BlockSpec((1,) + tbl.shape[1:], lambda h, bi: (h, 0, 0, 0)),
        ],
        out_specs=pl.BlockSpec((nb, s, HEAD_DIM), lambda h, bi: (bi, 0, h)),
        out_shape=jax.ShapeDtypeStruct((b, s, B_HEADS * HEAD_DIM), jnp.bfloat16),
        scratch_shapes=[f32_slab] * 16,
        compiler_params=_cparams(("parallel", "parallel")),
        name="attn_b",
    )(p3, p3, p3, tbl)


def _attn_c_kernel(q_ref, k_ref, v_ref, bank_ref, o_ref, *, starts, slabs):
    nq = NA_QROWS * GRID_W
    nk = NA_KROWS * GRID_W
    pw = 2 * GRID_W

    def one(bb, carry):
        for g, (ws, per_row) in enumerate(zip(starts, slabs)):
            qb = q_ref[bb, g * nq:(g + 1) * nq, :]
            kb = k_ref[bb, ws * GRID_W:ws * GRID_W + nk, :]
            vb = v_ref[bb, ws * GRID_W:ws * GRID_W + nk, :]
            s = lax.dot_general(qb, kb, _NT, preferred_element_type=jnp.float32)
            t_rows = []
            for rr, row_pairs in enumerate(per_row):
                pieces = []
                for p, slab in enumerate(row_pairs):
                    width = min(pw, nk - p * pw)
                    pieces.append(s[rr * GRID_W:(rr + 1) * GRID_W, p * pw:p * pw + width]
                                  + bank_ref[0, slab, :, :width])
                t_rows.append(jnp.concatenate(pieces, axis=1))
            t = jnp.concatenate(t_rows, axis=0)
            m = jnp.max(t, axis=-1, keepdims=True)
            e = jnp.exp2(t - m).astype(jnp.bfloat16)
            ol = jnp.dot(e, _with_ones(vb), preferred_element_type=jnp.float32)
            o_ref[bb, g * nq:(g + 1) * nq, :] = (ol[:, :HEAD_DIM] / ol[:, HEAD_DIM:]).astype(o_ref.dtype)
        return carry

    lax.fori_loop(0, q_ref.shape[0], one, 0)


def _attn_c(p3, tbl, starts, slabs):
    b, s, _ = p3.shape
    nb = math.gcd(b, ATTN_BATCH)
    return pl.pallas_call(
        functools.partial(_attn_c_kernel, starts=starts, slabs=slabs),
        grid=(C_HEADS, b // nb),
        in_specs=[
            pl.BlockSpec((nb, s, HEAD_DIM), lambda h, bi: (bi, 0, h)),
            pl.BlockSpec((nb, s, HEAD_DIM), lambda h, bi: (bi, 0, C_HEADS + h)),
            pl.BlockSpec((nb, s, HEAD_DIM), lambda h, bi: (bi, 0, 2 * C_HEADS + h)),
            pl.BlockSpec((1,) + tbl.shape[1:], lambda h, bi: (h, 0, 0, 0)),
        ],
        out_specs=pl.BlockSpec((nb, s, HEAD_DIM), lambda h, bi: (bi, 0, h)),
        out_shape=jax.ShapeDtypeStruct((b, s, C_HEADS * HEAD_DIM), jnp.bfloat16),
        compiler_params=_cparams(("parallel", "parallel")),
        name="attn_c",
    )(p3, p3, p3, tbl)


def _gate_out_kernel(*refs, n_y, n_z, final_norm):
    y_refs = refs[:n_y]
    z_refs = refs[n_y:n_y + n_z]
    w_ref, x_ref = refs[n_y + n_z:n_y + n_z + 2]
    o_ref = refs[-1]
    y = jnp.concatenate([r[...] for r in y_refs], axis=1)
    z = jnp.concatenate([r[...] for r in z_refs], axis=1)
    gated = y * (z / (1.0 + jnp.exp(-z)))
    out = x_ref[...] + jnp.dot(gated, w_ref[...], preferred_element_type=jnp.float32)
    if final_norm:
        g_ref = refs[-2]
        ms = jnp.mean(out * out, axis=-1, keepdims=True)
        out = out * lax.rsqrt(ms + RMS_EPS) * g_ref[...]
    o_ref[...] = out


def _gate_out(ys, p2d, z_col, z_blocks, w, x2d, g_final, *, tm, name):
    m, d = x2d.shape
    zw = d // z_blocks
    zb0 = z_col * HEAD_DIM // zw
    assert zb0 * zw == z_col * HEAD_DIM
    in_specs = [pl.BlockSpec((tm, y.shape[1]), lambda i: (i, 0)) for y in ys]
    in_specs += [pl.BlockSpec((tm, zw), functools.partial(lambda i, c: (i, c), c=zb0 + c)) for c in range(z_blocks)]
    in_specs += [pl.BlockSpec(w.shape, lambda i: (0, 0)), pl.BlockSpec((tm, d), lambda i: (i, 0))]
    args = list(ys) + [p2d] * z_blocks + [w, x2d]
    if g_final is not None:
        in_specs.append(pl.BlockSpec((1, d), lambda i: (0, 0)))
        args.append(g_final.reshape(1, d))
    return pl.pallas_call(
        functools.partial(_gate_out_kernel, n_y=len(ys), n_z=z_blocks, final_norm=g_final is not None),
        grid=(m // tm,),
        in_specs=in_specs,
        out_specs=pl.BlockSpec((tm, d), lambda i: (i, 0)),
        out_shape=jax.ShapeDtypeStruct((m, d), jnp.float32),
        compiler_params=_cparams(("parallel",)),
        name=name,
    )(*args)


def kernel(x, ln_ab, w_in_ab, sink_a, w_out_ab, ln_c, w_in_c, rpb_c, w_out_c, ln_f):
    b, s, d = x.shape
    assert s == 2048 and d == 2048, "tiling below is written for SEQ = D_MODEL = 2048"
    assert ln_ab.shape[0] == 1 and ln_c.shape[0] == 1, "depth-2 trunk: one layer of each kind"
    m = b * s
    bf16 = jnp.bfloat16
    x2d = x.reshape(m, d)

    slopes = _alibi_slopes(A_Q_HEADS + B_HEADS)
    aw = A_HALF_WINDOW
    tbl_a = jnp.stack([_band_bias(aw, 3 * aw, off, aw, 1, slopes[:A_Q_HEADS]) for off in (0, aw, 2 * aw)],
                      axis=1)
    qb, kw = 2 * B_HALF, 4 * B_HALF
    tbl_b = jnp.stack(
        [_band_bias(qb, kw, off, B_HALF, dil, slopes[A_Q_HEADS:]) for dil in (1, 4) for off in (0, B_HALF, 2 * B_HALF)]
        + [_band_bias(qb, kw, 0, B_HALF, B_CONFIGS[2][1], slopes[A_Q_HEADS:])], axis=1)
    na_starts, na_slabs, na_pairs = _na_plan(s // GRID_W)
    tbl_c = _na_bias_bank(rpb_c[0], na_pairs)

    hd = HEAD_DIM
    q_cols_ab = ((A_Q_COL * hd, A_K_COL * hd), (B_Q_COL * hd, B_K_COL * hd))
    p = _norm_proj(x2d, ln_ab[0], w_in_ab[0].astype(bf16), q_cols_ab, tm=512, tn=3328, name="norm_proj_ab")
    p3 = p.reshape(b, s, p.shape[1])
    ya = _attn_a(p3, sink_a[0], tbl_a)
    yb = _attn_b(p3, tbl_b)
    x1 = _gate_out([ya.reshape(m, -1), yb.reshape(m, -1)], p, Z_AB_COL, 4, w_out_ab[0].astype(bf16), x2d, None,
                   tm=512, name="gate_out_ab")

    pc = _norm_proj(x1, ln_c[0], w_in_c[0].astype(bf16), ((0, C_HEADS * hd),), tm=1024, tn=2048, name="norm_proj_c")
    yc = _attn_c(pc.reshape(b, s, pc.shape[1]), tbl_c, na_starts, na_slabs)
    out = _gate_out([yc.reshape(m, -1)], pc, 3 * C_HEADS, 1, w_out_c[0].astype(bf16), x1, ln_f,
                    tm=512, name="gate_out_c")
    return out.reshape(b, s, d)
```

```python
import functools
import math

import numpy as np
import jax
import jax.numpy as jnp
from jax import lax
from jax.experimental import pallas as pl
from jax.experimental.pallas import tpu as pltpu

HEAD_DIM = 128
A_Q_HEADS = 8
A_KV_HEADS = 2
A_GROUP = A_Q_HEADS // A_KV_HEADS
A_HALF_WINDOW = 128
B_HEADS = 8
B_CONFIGS = ((128, 1), (512, 4), (2048, 16))
B_HALF = 64
C_HEADS = 16
GRID_W = 64
NA_KH = 8
NA_KW = 16
RMS_EPS = 1e-5
NEG = -1e30
LOG2E = 1.4426950408889634
QK_SCALE = HEAD_DIM ** -0.5 * LOG2E

A_Q_COL = 0
A_K_COL = A_Q_HEADS
A_V_COL = A_K_COL + A_KV_HEADS
B_Q_COL = A_V_COL + A_KV_HEADS
B_K_COL = B_Q_COL + B_HEADS
B_V_COL = B_K_COL + B_HEADS
Z_AB_COL = B_V_COL + B_HEADS

NA_QROWS = 2
NA_KROWS = NA_QROWS + NA_KH - 1

A_UNROLL = 8
B_UNROLL = 16
ATTN_BATCH = 4

VMEM_LIMIT = 56 * 1024 * 1024

_NT = (((1,), (1,)), ((), ()))


def _cparams(sem):
    return pltpu.CompilerParams(dimension_semantics=sem, vmem_limit_bytes=VMEM_LIMIT)


def _edge_variant(i, n):
    return jnp.where(i == 0, 0, jnp.where(i == n - 1, 2, 1))


def _with_ones(vb):
    return jnp.concatenate([vb, jnp.ones_like(vb)], axis=1)


def _norm_proj_kernel(x_ref, g_ref, w_ref, cs_ref, o_ref):
    x = x_ref[...]
    ms = jnp.mean(x * x, axis=-1, keepdims=True)
    hn = (x * lax.rsqrt(ms + RMS_EPS) * g_ref[...]).astype(jnp.bfloat16)
    acc = jnp.dot(hn, w_ref[...], preferred_element_type=jnp.float32)
    o_ref[...] = (acc * cs_ref[...]).astype(o_ref.dtype)


def _norm_proj(x2d, g, w, q_cols, *, tm, name):
    m, d = x2d.shape
    n = w.shape[1]
    cs = np.ones((1, n), np.float32)
    for lo, hi in q_cols:
        cs[:, lo:hi] = QK_SCALE
    return pl.pallas_call(
        _norm_proj_kernel,
        grid=(m // tm,),
        in_specs=[
            pl.BlockSpec((tm, d), lambda i: (i, 0)),
            pl.BlockSpec((1, d), lambda i: (0, 0)),
            pl.BlockSpec((d, n), lambda i: (0, 0), pipeline_mode=pl.Buffered(1)),
            pl.BlockSpec((1, n), lambda i: (0, 0)),
        ],
        out_specs=pl.BlockSpec((tm, n), lambda i: (i, 0)),
        out_shape=jax.ShapeDtypeStruct((m, n), jnp.bfloat16),
        compiler_params=_cparams(("parallel",)),
        name=name,
    )(x2d, g.reshape(1, d), w, jnp.asarray(cs))


def _alibi_slopes(n):
    return jnp.exp2(-8.0 * jnp.arange(1, n + 1, dtype=jnp.float32) / n)


def _band_bias(nq, nk, off, half, dist_scale, slopes):
    rel = np.arange(nk)[None, :] - np.arange(nq)[:, None] - off
    valid = np.abs(rel) <= half
    dist = jnp.asarray((np.abs(rel) * dist_scale).astype(np.float32))
    bias = -slopes[:, None, None] * dist
    return jnp.where(valid[None], bias * LOG2E, NEG)


def _na_plan(rows):
    masked = 2 * NA_KH - 1
    starts, slabs, pairs = [], [], []
    for g in range(rows // NA_QROWS):
        ws = int(np.clip(g * NA_QROWS - NA_KH // 2, 0, rows - NA_KROWS))
        per_row = []
        for rr in range(NA_QROWS):
            r = g * NA_QROWS + rr
            r0 = int(np.clip(r - NA_KH // 2, 0, rows - NA_KH))
            assert ws <= r0 and r0 + NA_KH <= ws + NA_KROWS
            ids = [kr - r + NA_KH - 1 if r0 <= kr < r0 + NA_KH else masked for kr in range(ws, ws + NA_KROWS)]
            ids += [masked] * (len(ids) % 2)
            row_pairs = []
            for p in range(0, len(ids), 2):
                pair = (ids[p], ids[p + 1])
                if pair not in pairs:
                    pairs.append(pair)
                row_pairs.append(pairs.index(pair))
            per_row.append(tuple(row_pairs))
        starts.append(ws)
        slabs.append(tuple(per_row))
    return tuple(starts), tuple(slabs), tuple(pairs)


def _na_bias_bank(rpb, pairs):
    h, n_dr, n_dc = rpb.shape
    cq = np.arange(GRID_W)
    c0 = np.clip(cq - NA_KW // 2, 0, GRID_W - NA_KW)
    ck = np.arange(GRID_W)
    col_valid = (ck[None, :] >= c0[:, None]) & (ck[None, :] < c0[:, None] + NA_KW)
    dc = ck[None, :] - cq[:, None] + NA_KW - 1
    onehot = (dc[None] == np.arange(n_dc)[:, None, None]) & col_valid[None]
    rhs = np.zeros((2, n_dc, GRID_W, 2, GRID_W), np.float32)
    for j in range(2):
        rhs[j, :, :, j, :] = onehot
    sel = np.asarray(pairs, np.int32)
    rp = jnp.concatenate([rpb * LOG2E, jnp.zeros((h, 1, n_dc), jnp.float32)], axis=1)
    lhs = jnp.take(rp, jnp.asarray(sel.reshape(-1)), axis=1).reshape(h, len(pairs), 2 * n_dc)
    bank = jnp.einsum("hnx,xqk->hnqk", lhs, jnp.asarray(rhs.reshape(2 * n_dc, GRID_W, 2 * GRID_W)),
                      precision=lax.Precision.HIGHEST)
    valid = (sel != n_dr)[:, None, :, None] & col_valid[None, :, None, :]
    return jnp.where(jnp.asarray(valid.reshape(len(pairs), GRID_W, 2 * GRID_W))[None], bank, NEG)


def _attn_a_kernel(sink_ref, q_ref, k_ref, v_ref, tbl_ref, o_ref, *, seq):
    blk = A_HALF_WINDOW
    nk = 3 * blk
    nblk = seq // blk
    g = pl.program_id(0)
    sinks = [sink_ref[g * A_GROUP + j] * LOG2E for j in range(A_GROUP)]

    def block(bb, i):
        qrow = pl.multiple_of(i * blk, blk)
        krow = pl.multiple_of(jnp.clip((i - 1) * blk, 0, seq - nk), blk)
        var = _edge_variant(i, nblk)
        q4 = q_ref[bb, pl.ds(qrow, blk), :]
        qs = jnp.concatenate([q4[:, j * HEAD_DIM:(j + 1) * HEAD_DIM] for j in range(A_GROUP)], axis=0)
        kb = k_ref[bb, pl.ds(krow, nk), :]
        vb = v_ref[bb, pl.ds(krow, nk), :]
        s = lax.dot_general(qs, kb, _NT, preferred_element_type=jnp.float32)
        es, ls = [], []
        for j in range(A_GROUP):
            t = s[j * blk:(j + 1) * blk, :] + tbl_ref[j, var]
            m = jnp.maximum(jnp.max(t, axis=-1, keepdims=True), sinks[j])
            e = jnp.exp2(t - m)
            ls.append(jnp.sum(e, axis=-1, keepdims=True) + jnp.exp2(sinks[j] - m))
            es.append(e.astype(jnp.bfloat16))
        o = jnp.dot(jnp.concatenate(es, axis=0), vb, preferred_element_type=jnp.float32)
        out = jnp.concatenate([o[j * blk:(j + 1) * blk] / ls[j] for j in range(A_GROUP)], axis=1)
        o_ref[bb, pl.ds(qrow, blk), :] = out.astype(o_ref.dtype)

    iters = nblk // A_UNROLL

    def body(n, carry):
        bb, it = n // iters, n % iters
        for u in range(A_UNROLL):
            block(bb, it * A_UNROLL + u)
        return carry

    lax.fori_loop(0, q_ref.shape[0] * iters, body, 0)


def _attn_a(p3, sink, tbl):
    b, s, _ = p3.shape
    gw = A_GROUP * HEAD_DIM
    nb = math.gcd(b, ATTN_BATCH)
    return pl.pallas_call(
        functools.partial(_attn_a_kernel, seq=s),
        grid=(A_KV_HEADS, b // nb),
        in_specs=[
            pl.BlockSpec(memory_space=pltpu.SMEM),
            pl.BlockSpec((nb, s, gw), lambda g, bi: (bi, 0, g)),
            pl.BlockSpec((nb, s, HEAD_DIM), lambda g, bi: (bi, 0, A_K_COL + g)),
            pl.BlockSpec((nb, s, HEAD_DIM), lambda g, bi: (bi, 0, A_V_COL + g)),
            pl.BlockSpec((A_GROUP,) + tbl.shape[1:], lambda g, bi: (g, 0, 0, 0)),
        ],
        out_specs=pl.BlockSpec((nb, s, gw), lambda g, bi: (bi, 0, g)),
        out_shape=jax.ShapeDtypeStruct((b, s, A_Q_HEADS * HEAD_DIM), jnp.bfloat16),
        compiler_params=_cparams(("parallel", "parallel")),
        name="attn_a",
    )(sink, p3, p3, p3, tbl)


def _attn_b_kernel(*refs, seq):
    batch_rows = refs[0].shape[0]
    lax.fori_loop(0, batch_rows, functools.partial(_attn_b_one, refs=refs, seq=seq), 0)


def _attn_b_one(bb, carry, *, refs, seq):
    (q_ref, k_ref, v_ref, tbl_ref, o_ref, qf, kf, vf, q4, k4, v4, of,
     acc0, acc1, acc2, m0, m1, m2, l0, l1, l2) = refs
    qb_rows = 2 * B_HALF
    kw = 4 * B_HALF
    fold = B_CONFIGS[1][1]

    def attend(qb, kb, vb, tb):
        s = lax.dot_general(qb, kb, _NT, preferred_element_type=jnp.float32)
        t = s + tb
        m = jnp.max(t, axis=-1, keepdims=True)
        e = jnp.exp2(t - m).astype(jnp.bfloat16)
        al = jnp.dot(e, _with_ones(vb), preferred_element_type=jnp.float32)
        return al[:, :HEAD_DIM], jnp.broadcast_to(m, (qb.shape[0], HEAD_DIM)), al[:, HEAD_DIM:]

    nblk = seq // qb_rows

    def nat_block(i):
        qrow = pl.multiple_of(i * qb_rows, qb_rows)
        krow = pl.multiple_of(jnp.clip(i * qb_rows - B_HALF, 0, seq - kw), B_HALF)
        a, m, l = attend(q_ref[bb, pl.ds(qrow, qb_rows), :], k_ref[bb, pl.ds(krow, kw), :],
                         v_ref[bb, pl.ds(krow, kw), :], tbl_ref[0, _edge_variant(i, nblk)])
        acc0[pl.ds(qrow, qb_rows), :] = a
        m0[pl.ds(qrow, qb_rows), :] = m
        l0[pl.ds(qrow, qb_rows), :] = l

    def nat_body(it, carry):
        for u in range(B_UNROLL):
            nat_block(it * B_UNROLL + u)
        return carry

    lax.fori_loop(0, nblk // B_UNROLL, nat_body, 0)

    qf[...] = q_ref[bb].astype(jnp.float32)
    kf[...] = k_ref[bb].astype(jnp.float32)
    vf[...] = v_ref[bb].astype(jnp.float32)

    def dilated(d, var0, acc, mm, ll):
        ln = seq // d
        nb = ln // qb_rows
        per_iter = max(B_UNROLL // nb, 1)

        def residue(r):
            if d == fold:
                cls = [src[pl.ds(r, ln, stride=d), :] for src in (qf, kf, vf)]
                for dst, val in zip((q4, k4, v4), cls):
                    dst[pl.ds(pl.multiple_of(r * ln, ln), ln), :] = val
            else:
                first = (r % fold) * (seq // fold) + r // fold
                cls = [src[pl.ds(first, ln, stride=d // fold), :] for src in (q4, k4, v4)]
            qr, kr, vr = (val.astype(jnp.bfloat16) for val in cls)
            for i in range(nb):
                if nb == 1:
                    k0, width, var = 0, ln, var0
                else:
                    k0 = min(max(i * qb_rows - B_HALF, 0), ln - kw)
                    width = kw
                    var = var0 + (0 if i == 0 else (2 if i == nb - 1 else 1))
                a, m, l = attend(qr[i * qb_rows:(i + 1) * qb_rows], kr[k0:k0 + width], vr[k0:k0 + width],
                                 tbl_ref[0, var, :, :width])
                start = (r % fold) * (seq // fold) + r // fold + (d // fold) * qb_rows * i
                if d == fold:
                    rows = pl.ds(pl.multiple_of(start, qb_rows), qb_rows)
                else:
                    rows = pl.ds(start, qb_rows, stride=d // fold)
                acc[rows, :] = a
                mm[rows, :] = m
                ll[rows, :] = l

        def body(it, carry):
            for u in range(per_iter):
                residue(it * per_iter + u)
            return carry

        lax.fori_loop(0, d // per_iter, body, 0)

    dilated(B_CONFIGS[1][1], 3, acc1, m1, l1)
    dilated(B_CONFIGS[2][1], 6, acc2, m2, l2)

    merge_rows = 2 * qb_rows
    chunks_per_class = seq // fold // merge_rows

    def merge(c, carry):
        rows = pl.ds(pl.multiple_of(c * merge_rows, merge_rows), merge_rows)
        nat = pl.ds(c // chunks_per_class + fold * merge_rows * (c % chunks_per_class), merge_rows, stride=fold)
        ma, mb, mc = m0[nat, :], m1[rows, :], m2[rows, :]
        mx = jnp.maximum(jnp.maximum(ma, mb), mc)
        wa, wb, wc = jnp.exp2(ma - mx), jnp.exp2(mb - mx), jnp.exp2(mc - mx)
        num = wa * acc0[nat, :] + wb * acc1[rows, :] + wc * acc2[rows, :]
        den = wa * l0[nat, :] + wb * l1[rows, :] + wc * l2[rows, :]
        of[nat, :] = num / den
        return carry

    lax.fori_loop(0, seq // merge_rows, merge, 0)
    o_ref[bb] = of[...].astype(o_ref.dtype)
    return carry


def _attn_b(p3, tbl):
    b, s, _ = p3.shape
    f32_slab = pltpu.VMEM((s, HEAD_DIM), jnp.float32)
    nb = math.gcd(b, ATTN_BATCH)
    return pl.pallas_call(
        functools.partial(_attn_b_kernel, seq=s),
        grid=(B_HEADS, b // nb),
        in_specs=[
            pl.BlockSpec((nb, s, HEAD_DIM), lambda h, bi: (bi, 0, B_Q_COL + h)),
            pl.BlockSpec((nb, s, HEAD_DIM), lambda h, bi: (bi, 0, B_K_COL + h)),
            pl.BlockSpec((nb, s, HEAD_DIM), lambda h, bi: (bi, 0, B_V_COL + h)),
            pl.BlockSpec((1,) + tbl.shape[1:], lambda h, bi: (h, 0, 0, 0)),
        ],
        out_specs=pl.BlockSpec((nb, s, HEAD_DIM), lambda h, bi: (bi, 0, h)),
        out_shape=jax.ShapeDtypeStruct((b, s, B_HEADS * HEAD_DIM), jnp.bfloat16),
        scratch_shapes=[f32_slab] * 16,
        compiler_params=_cparams(("parallel", "parallel")),
        name="attn_b",
    )(p3, p3, p3, tbl)


def _attn_c_kernel(q_ref, k_ref, v_ref, bank_ref, o_ref, *, starts, slabs):
    nq = NA_QROWS * GRID_W
    nk = NA_KROWS * GRID_W
    pw = 2 * GRID_W

    def one(bb, carry):
        for g, (ws, per_row) in enumerate(zip(starts, slabs)):
            qb = q_ref[bb, g * nq:(g + 1) * nq, :]
            kb = k_ref[bb, ws * GRID_W:ws * GRID_W + nk, :]
            vb = v_ref[bb, ws * GRID_W:ws * GRID_W + nk, :]
            s = lax.dot_general(qb, kb, _NT, preferred_element_type=jnp.float32)
            t_rows = []
            for rr, row_pairs in enumerate(per_row):
                pieces = []
                for p, slab in enumerate(row_pairs):
                    width = min(pw, nk - p * pw)
                    pieces.append(s[rr * GRID_W:(rr + 1) * GRID_W, p * pw:p * pw + width]
                                  + bank_ref[0, slab, :, :width])
                t_rows.append(jnp.concatenate(pieces, axis=1))
            t = jnp.concatenate(t_rows, axis=0)
            m = jnp.max(t, axis=-1, keepdims=True)
            e = jnp.exp2(t - m).astype(jnp.bfloat16)
            ol = jnp.dot(e, _with_ones(vb), preferred_element_type=jnp.float32)
            o_ref[bb, g * nq:(g + 1) * nq, :] = (ol[:, :HEAD_DIM] / ol[:, HEAD_DIM:]).astype(o_ref.dtype)
        return carry

    lax.fori_loop(0, q_ref.shape[0], one, 0)


def _attn_c(p3, tbl, starts, slabs):
    b, s, _ = p3.shape
    nb = math.gcd(b, ATTN_BATCH)
    return pl.pallas_call(
        functools.partial(_attn_c_kernel, starts=starts, slabs=slabs),
        grid=(C_HEADS, b // nb),
        in_specs=[
            pl.BlockSpec((nb, s, HEAD_DIM), lambda h, bi: (bi, 0, h)),
            pl.BlockSpec((nb, s, HEAD_DIM), lambda h, bi: (bi, 0, C_HEADS + h)),
            pl.BlockSpec((nb, s, HEAD_DIM), lambda h, bi: (bi, 0, 2 * C_HEADS + h)),
            pl.BlockSpec((1,) + tbl.shape[1:], lambda h, bi: (h, 0, 0, 0)),
        ],
        out_specs=pl.BlockSpec((nb, s, HEAD_DIM), lambda h, bi: (bi, 0, h)),
        out_shape=jax.ShapeDtypeStruct((b, s, C_HEADS * HEAD_DIM), jnp.bfloat16),
        compiler_params=_cparams(("parallel", "parallel")),
        name="attn_c",
    )(p3, p3, p3, tbl)


def _gate_out_kernel(*refs, n_y, n_z, final_norm):
    y_refs = refs[:n_y]
    z_refs = refs[n_y:n_y + n_z]
    w_ref, x_ref = refs[n_y + n_z:n_y + n_z + 2]
    o_ref = refs[-1]
    y = jnp.concatenate([r[...] for r in y_refs], axis=1)
    z = jnp.concatenate([r[...] for r in z_refs], axis=1)
    gated = y * (z / (1.0 + jnp.exp(-z)))
    out = x_ref[...] + jnp.dot(gated, w_ref[...], preferred_element_type=jnp.float32)
    if final_norm:
        g_ref = refs[-2]
        ms = jnp.mean(out * out, axis=-1, keepdims=True)
        out = out * lax.rsqrt(ms + RMS_EPS) * g_ref[...]
    o_ref[...] = out


def _gate_out(ys, p2d, z_col, z_blocks, w, x2d, g_final, *, tm, name):
    m, d = x2d.shape
    zw = d // z_blocks
    zb0 = z_col * HEAD_DIM // zw
    assert zb0 * zw == z_col * HEAD_DIM
    in_specs = [pl.BlockSpec((tm, y.shape[1]), lambda i: (i, 0)) for y in ys]
    in_specs += [pl.BlockSpec((tm, zw), functools.partial(lambda i, c: (i, c), c=zb0 + c)) for c in range(z_blocks)]
    in_specs += [pl.BlockSpec(w.shape, lambda i: (0, 0)), pl.BlockSpec((tm, d), lambda i: (i, 0))]
    args = list(ys) + [p2d] * z_blocks + [w, x2d]
    if g_final is not None:
        in_specs.append(pl.BlockSpec((1, d), lambda i: (0, 0)))
        args.append(g_final.reshape(1, d))
    return pl.pallas_call(
        functools.partial(_gate_out_kernel, n_y=len(ys), n_z=z_blocks, final_norm=g_final is not None),
        grid=(m // tm,),
        in_specs=in_specs,
        out_specs=pl.BlockSpec((tm, d), lambda i: (i, 0)),
        out_shape=jax.ShapeDtypeStruct((m, d), jnp.float32),
        compiler_params=_cparams(("parallel",)),
        name=name,
    )(*args)


def kernel(x, ln_ab, w_in_ab, sink_a, w_out_ab, ln_c, w_in_c, rpb_c, w_out_c, ln_f):
    b, s, d = x.shape
    assert s == 2048 and d == 2048, "tiling below is written for SEQ = D_MODEL = 2048"
    assert ln_ab.shape[0] == 1 and ln_c.shape[0] == 1, "depth-2 trunk: one layer of each kind"
    m = b * s
    bf16 = jnp.bfloat16
    x2d = x.reshape(m, d)

    slopes = _alibi_slopes(A_Q_HEADS + B_HEADS)
    aw = A_HALF_WINDOW
    tbl_a = jnp.stack([_band_bias(aw, 3 * aw, off, aw, 1, slopes[:A_Q_HEADS]) for off in (0, aw, 2 * aw)],
                      axis=1)
    qb, kw = 2 * B_HALF, 4 * B_HALF
    tbl_b = jnp.stack(
        [_band_bias(qb, kw, off, B_HALF, dil, slopes[A_Q_HEADS:]) for dil in (1, 4) for off in (0, B_HALF, 2 * B_HALF)]
        + [_band_bias(qb, kw, 0, B_HALF, B_CONFIGS[2][1], slopes[A_Q_HEADS:])], axis=1)
    na_starts, na_slabs, na_pairs = _na_plan(s // GRID_W)
    tbl_c = _na_bias_bank(rpb_c[0], na_pairs)

    hd = HEAD_DIM
    q_cols_ab = ((A_Q_COL * hd, A_K_COL * hd), (B_Q_COL * hd, B_K_COL * hd))
    p = _norm_proj(x2d, ln_ab[0], w_in_ab[0].astype(bf16), q_cols_ab, tm=512, name="norm_proj_ab")
    p3 = p.reshape(b, s, p.shape[1])
    ya = _attn_a(p3, sink_a[0], tbl_a)
    yb = _attn_b(p3, tbl_b)
    x1 = _gate_out([ya.reshape(m, -1), yb.reshape(m, -1)], p, Z_AB_COL, 4, w_out_ab[0].astype(bf16), x2d, None,
                   tm=512, name="gate_out_ab")

    pc = _norm_proj(x1, ln_c[0], w_in_c[0].astype(bf16), ((0, C_HEADS * hd),), tm=256, name="norm_proj_c")
    yc = _attn_c(pc.reshape(b, s, pc.shape[1]), tbl_c, na_starts, na_slabs)
    out = _gate_out([yc.reshape(m, -1)], pc, 3 * C_HEADS, 1, w_out_c[0].astype(bf16), x1, ln_f,
                    tm=512, name="gate_out_c")
    return out.reshape(b, s, d)
```

```python
import functools
import math

import numpy as np
import jax
import jax.numpy as jnp
from jax import lax
from jax.experimental import pallas as pl
from jax.experimental.pallas import tpu as pltpu

HEAD_DIM = 128
A_Q_HEADS = 8
A_KV_HEADS = 2
A_GROUP = A_Q_HEADS // A_KV_HEADS
A_HALF_WINDOW = 128
B_HEADS = 8
B_CONFIGS = ((128, 1), (512, 4), (2048, 16))
B_HALF = 64
C_HEADS = 16
GRID_W = 64
NA_KH = 8
NA_KW = 16
RMS_EPS = 1e-5
NEG = -1e30
LOG2E = 1.4426950408889634
QK_SCALE = HEAD_DIM ** -0.5 * LOG2E

A_Q_COL = 0
A_K_COL = A_Q_HEADS
A_V_COL = A_K_COL + A_KV_HEADS
B_Q_COL = A_V_COL + A_KV_HEADS
B_K_COL = B_Q_COL + B_HEADS
B_V_COL = B_K_COL + B_HEADS
Z_AB_COL = B_V_COL + B_HEADS

NA_QROWS = 2
NA_KROWS = NA_QROWS + NA_KH - 1

A_UNROLL = 8
B_UNROLL = 16
ATTN_BATCH = 4

VMEM_LIMIT = 60 * 1024 * 1024

_NT = (((1,), (1,)), ((), ()))


def _cparams(sem):
    return pltpu.CompilerParams(dimension_semantics=sem, vmem_limit_bytes=VMEM_LIMIT)


def _edge_variant(i, n):
    return jnp.where(i == 0, 0, jnp.where(i == n - 1, 2, 1))


def _with_ones(vb):
    return jnp.concatenate([vb, jnp.ones_like(vb)], axis=1)


def _norm_proj_kernel(x_ref, g_ref, w_ref, cs_ref, o_ref):
    x = x_ref[...]
    ms = jnp.mean(x * x, axis=-1, keepdims=True)
    hn = (x * lax.rsqrt(ms + RMS_EPS) * g_ref[...]).astype(jnp.bfloat16)
    acc = jnp.dot(hn, w_ref[...], preferred_element_type=jnp.float32)
    o_ref[...] = (acc * cs_ref[...]).astype(o_ref.dtype)


def _norm_proj(x2d, g, w, q_cols, *, tm, name):
    m, d = x2d.shape
    n = w.shape[1]
    cs = np.ones((1, n), np.float32)
    for lo, hi in q_cols:
        cs[:, lo:hi] = QK_SCALE
    return pl.pallas_call(
        _norm_proj_kernel,
        grid=(m // tm,),
        in_specs=[
            pl.BlockSpec((tm, d), lambda i: (i, 0)),
            pl.BlockSpec((1, d), lambda i: (0, 0)),
            pl.BlockSpec((d, n), lambda i: (0, 0), pipeline_mode=pl.Buffered(1)),
            pl.BlockSpec((1, n), lambda i: (0, 0)),
        ],
        out_specs=pl.BlockSpec((tm, n), lambda i: (i, 0)),
        out_shape=jax.ShapeDtypeStruct((m, n), jnp.bfloat16),
        compiler_params=_cparams(("parallel",)),
        name=name,
    )(x2d, g.reshape(1, d), w, jnp.asarray(cs))


def _alibi_slopes(n):
    return jnp.exp2(-8.0 * jnp.arange(1, n + 1, dtype=jnp.float32) / n)


def _band_bias(nq, nk, off, half, dist_scale, slopes):
    rel = np.arange(nk)[None, :] - np.arange(nq)[:, None] - off
    valid = np.abs(rel) <= half
    dist = jnp.asarray((np.abs(rel) * dist_scale).astype(np.float32))
    bias = -slopes[:, None, None] * dist
    return jnp.where(valid[None], bias * LOG2E, NEG)


def _na_plan(rows):
    masked = 2 * NA_KH - 1
    starts, slabs, pairs = [], [], []
    for g in range(rows // NA_QROWS):
        ws = int(np.clip(g * NA_QROWS - NA_KH // 2, 0, rows - NA_KROWS))
        per_row = []
        for rr in range(NA_QROWS):
            r = g * NA_QROWS + rr
            r0 = int(np.clip(r - NA_KH // 2, 0, rows - NA_KH))
            assert ws <= r0 and r0 + NA_KH <= ws + NA_KROWS
            ids = [kr - r + NA_KH - 1 if r0 <= kr < r0 + NA_KH else masked for kr in range(ws, ws + NA_KROWS)]
            ids += [masked] * (len(ids) % 2)
            row_pairs = []
            for p in range(0, len(ids), 2):
                pair = (ids[p], ids[p + 1])
                if pair not in pairs:
                    pairs.append(pair)
                row_pairs.append(pairs.index(pair))
            per_row.append(tuple(row_pairs))
        starts.append(ws)
        slabs.append(tuple(per_row))
    return tuple(starts), tuple(slabs), tuple(pairs)


def _na_bias_bank(rpb, pairs):
    h, n_dr, n_dc = rpb.shape
    cq = np.arange(GRID_W)
    c0 = np.clip(cq - NA_KW // 2, 0, GRID_W - NA_KW)
    ck = np.arange(GRID_W)
    col_valid = (ck[None, :] >= c0[:, None]) & (ck[None, :] < c0[:, None] + NA_KW)
    dc = ck[None, :] - cq[:, None] + NA_KW - 1
    onehot = (dc[None] == np.arange(n_dc)[:, None, None]) & col_valid[None]
    rhs = np.zeros((2, n_dc, GRID_W, 2, GRID_W), np.float32)
    for j in range(2):
        rhs[j, :, :, j, :] = onehot
    sel = np.asarray(pairs, np.int32)
    rp = jnp.concatenate([rpb * LOG2E, jnp.zeros((h, 1, n_dc), jnp.float32)], axis=1)
    lhs = jnp.take(rp, jnp.asarray(sel.reshape(-1)), axis=1).reshape(h, len(pairs), 2 * n_dc)
    bank = jnp.einsum("hnx,xqk->hnqk", lhs, jnp.asarray(rhs.reshape(2 * n_dc, GRID_W, 2 * GRID_W)),
                      precision=lax.Precision.HIGHEST)
    valid = (sel != n_dr)[:, None, :, None] & col_valid[None, :, None, :]
    return jnp.where(jnp.asarray(valid.reshape(len(pairs), GRID_W, 2 * GRID_W))[None], bank, NEG)


def _attn_a_kernel(sink_ref, q_ref, k_ref, v_ref, tbl_ref, o_ref, *, seq):
    blk = A_HALF_WINDOW
    nk = 3 * blk
    nblk = seq // blk
    g = pl.program_id(0)
    sinks = [sink_ref[g * A_GROUP + j] * LOG2E for j in range(A_GROUP)]

    def block(bb, i):
        qrow = pl.multiple_of(i * blk, blk)
        krow = pl.multiple_of(jnp.clip((i - 1) * blk, 0, seq - nk), blk)
        var = _edge_variant(i, nblk)
        q4 = q_ref[bb, pl.ds(qrow, blk), :]
        qs = jnp.concatenate([q4[:, j * HEAD_DIM:(j + 1) * HEAD_DIM] for j in range(A_GROUP)], axis=0)
        kb = k_ref[bb, pl.ds(krow, nk), :]
        vb = v_ref[bb, pl.ds(krow, nk), :]
        s = lax.dot_general(qs, kb, _NT, preferred_element_type=jnp.float32)
        es, ls = [], []
        for j in range(A_GROUP):
            t = s[j * blk:(j + 1) * blk, :] + tbl_ref[j, var]
            m = jnp.maximum(jnp.max(t, axis=-1, keepdims=True), sinks[j])
            e = jnp.exp2(t - m)
            ls.append(jnp.sum(e, axis=-1, keepdims=True) + jnp.exp2(sinks[j] - m))
            es.append(e.astype(jnp.bfloat16))
        o = jnp.dot(jnp.concatenate(es, axis=0), vb, preferred_element_type=jnp.float32)
        out = jnp.concatenate([o[j * blk:(j + 1) * blk] / ls[j] for j in range(A_GROUP)], axis=1)
        o_ref[bb, pl.ds(qrow, blk), :] = out.astype(o_ref.dtype)

    iters = nblk // A_UNROLL

    def body(n, carry):
        bb, it = n // iters, n % iters
        for u in range(A_UNROLL):
            block(bb, it * A_UNROLL + u)
        return carry

    lax.fori_loop(0, q_ref.shape[0] * iters, body, 0)


def _attn_a(p3, sink, tbl):
    b, s, _ = p3.shape
    gw = A_GROUP * HEAD_DIM
    nb = math.gcd(b, ATTN_BATCH)
    return pl.pallas_call(
        functools.partial(_attn_a_kernel, seq=s),
        grid=(A_KV_HEADS, b // nb),
        in_specs=[
            pl.BlockSpec(memory_space=pltpu.SMEM),
            pl.BlockSpec((nb, s, gw), lambda g, bi: (bi, 0, g)),
            pl.BlockSpec((nb, s, HEAD_DIM), lambda g, bi: (bi, 0, A_K_COL + g)),
            pl.BlockSpec((nb, s, HEAD_DIM), lambda g, bi: (bi, 0, A_V_COL + g)),
            pl.BlockSpec((A_GROUP,) + tbl.shape[1:], lambda g, bi: (g, 0, 0, 0)),
        ],
        out_specs=pl.BlockSpec((nb, s, gw), lambda g, bi: (bi, 0, g)),
        out_shape=jax.ShapeDtypeStruct((b, s, A_Q_HEADS * HEAD_DIM), jnp.bfloat16),
        compiler_params=_cparams(("parallel", "parallel")),
        name="attn_a",
    )(sink, p3, p3, p3, tbl)


def _attn_b_kernel(*refs, seq):
    batch_rows = refs[0].shape[0]
    lax.fori_loop(0, batch_rows, functools.partial(_attn_b_one, refs=refs, seq=seq), 0)


def _attn_b_one(bb, carry, *, refs, seq):
    (q_ref, k_ref, v_ref, tbl_ref, o_ref, qf, kf, vf, q4, k4, v4, of,
     acc0, acc1, acc2, m0, m1, m2, l0, l1, l2) = refs
    qb_rows = 2 * B_HALF
    kw = 4 * B_HALF
    fold = B_CONFIGS[1][1]

    def attend(qb, kb, vb, tb):
        s = lax.dot_general(qb, kb, _NT, preferred_element_type=jnp.float32)
        t = s + tb
        m = jnp.max(t, axis=-1, keepdims=True)
        e = jnp.exp2(t - m).astype(jnp.bfloat16)
        al = jnp.dot(e, _with_ones(vb), preferred_element_type=jnp.float32)
        return al[:, :HEAD_DIM], jnp.broadcast_to(m, (qb.shape[0], HEAD_DIM)), al[:, HEAD_DIM:]

    nblk = seq // qb_rows

    def nat_block(i):
        qrow = pl.multiple_of(i * qb_rows, qb_rows)
        krow = pl.multiple_of(jnp.clip(i * qb_rows - B_HALF, 0, seq - kw), B_HALF)
        a, m, l = attend(q_ref[bb, pl.ds(qrow, qb_rows), :], k_ref[bb, pl.ds(krow, kw), :],
                         v_ref[bb, pl.ds(krow, kw), :], tbl_ref[0, _edge_variant(i, nblk)])
        acc0[pl.ds(qrow, qb_rows), :] = a
        m0[pl.ds(qrow, qb_rows), :] = m
        l0[pl.ds(qrow, qb_rows), :] = l

    def nat_body(it, carry):
        for u in range(B_UNROLL):
            nat_block(it * B_UNROLL + u)
        return carry

    lax.fori_loop(0, nblk // B_UNROLL, nat_body, 0)

    qf[...] = q_ref[bb].astype(jnp.float32)
    kf[...] = k_ref[bb].astype(jnp.float32)
    vf[...] = v_ref[bb].astype(jnp.float32)

    def dilated(d, var0, acc, mm, ll):
        ln = seq // d
        nb = ln // qb_rows
        per_iter = max(B_UNROLL // nb, 1)

        def residue(r):
            if d == fold:
                cls = [src[pl.ds(r, ln, stride=d), :] for src in (qf, kf, vf)]
                for dst, val in zip((q4, k4, v4), cls):
                    dst[pl.ds(pl.multiple_of(r * ln, ln), ln), :] = val
            else:
                first = (r % fold) * (seq // fold) + r // fold
                cls = [src[pl.ds(first, ln, stride=d // fold), :] for src in (q4, k4, v4)]
            qr, kr, vr = (val.astype(jnp.bfloat16) for val in cls)
            for i in range(nb):
                if nb == 1:
                    k0, width, var = 0, ln, var0
                else:
                    k0 = min(max(i * qb_rows - B_HALF, 0), ln - kw)
                    width = kw
                    var = var0 + (0 if i == 0 else (2 if i == nb - 1 else 1))
                a, m, l = attend(qr[i * qb_rows:(i + 1) * qb_rows], kr[k0:k0 + width], vr[k0:k0 + width],
                                 tbl_ref[0, var, :, :width])
                start = (r % fold) * (seq // fold) + r // fold + (d // fold) * qb_rows * i
                if d == fold:
                    rows = pl.ds(pl.multiple_of(start, qb_rows), qb_rows)
                else:
                    rows = pl.ds(start, qb_rows, stride=d // fold)
                acc[rows, :] = a
                mm[rows, :] = m
                ll[rows, :] = l

        def body(it, carry):
            for u in range(per_iter):
                residue(it * per_iter + u)
            return carry

        lax.fori_loop(0, d // per_iter, body, 0)

    dilated(B_CONFIGS[1][1], 3, acc1, m1, l1)
    dilated(B_CONFIGS[2][1], 6, acc2, m2, l2)

    merge_rows = 2 * qb_rows
    chunks_per_class = seq // fold // merge_rows

    def merge(c, carry):
        rows = pl.ds(pl.multiple_of(c * merge_rows, merge_rows), merge_rows)
        nat = pl.ds(c // chunks_per_class + fold * merge_rows * (c % chunks_per_class), merge_rows, stride=fold)
        ma, mb, mc = m0[nat, :], m1[rows, :], m2[rows, :]
        mx = jnp.maximum(jnp.maximum(ma, mb), mc)
        wa, wb, wc = jnp.exp2(ma - mx), jnp.exp2(mb - mx), jnp.exp2(mc - mx)
        num = wa * acc0[nat, :] + wb * acc1[rows, :] + wc * acc2[rows, :]
        den = wa * l0[nat, :] + wb * l1[rows, :] + wc * l2[rows, :]
        of[nat, :] = num / den
        return carry

    lax.fori_loop(0, seq // merge_rows, merge, 0)
    o_ref[bb] = of[...].astype(o_ref.dtype)
    return carry


def _attn_b(p3, tbl):
    b, s, _ = p3.shape
    f32_slab = pltpu.VMEM((s, HEAD_DIM), jnp.float32)
    nb = math.gcd(b, ATTN_BATCH)
    return pl.pallas_call(
        functools.partial(_attn_b_kernel, seq=s),
        grid=(B_HEADS, b // nb),
        in_specs=[
            pl.BlockSpec((nb, s, HEAD_DIM), lambda h, bi: (bi, 0, B_Q_COL + h)),
            pl.BlockSpec((nb, s, HEAD_DIM), lambda h, bi: (bi, 0, B_K_COL + h)),
            pl.BlockSpec((nb, s, HEAD_DIM), lambda h, bi: (bi, 0, B_V_COL + h)),
            pl.BlockSpec((1,) + tbl.shape[1:], lambda h, bi: (h, 0, 0, 0)),
        ],
        out_specs=pl.BlockSpec((nb, s, HEAD_DIM), lambda h, bi: (bi, 0, h)),
        out_shape=jax.ShapeDtypeStruct((b, s, B_HEADS * HEAD_DIM), jnp.bfloat16),
        scratch_shapes=[f32_slab] * 16,
        compiler_params=_cparams(("parallel", "parallel")),
        name="attn_b",
    )(p3, p3, p3, tbl)


def _attn_c_kernel(q_ref, k_ref, v_ref, bank_ref, o_ref, *, starts, slabs):
    nq = NA_QROWS * GRID_W
    nk = NA_KROWS * GRID_W
    pw = 2 * GRID_W

    def one(bb, carry):
        for g, (ws, per_row) in enumerate(zip(starts, slabs)):
            qb = q_ref[bb, g * nq:(g + 1) * nq, :]
            kb = k_ref[bb, ws * GRID_W:ws * GRID_W + nk, :]
            vb = v_ref[bb, ws * GRID_W:ws * GRID_W + nk, :]
            s = lax.dot_general(qb, kb, _NT, preferred_element_type=jnp.float32)
            t_rows = []
            for rr, row_pairs in enumerate(per_row):
                pieces = []
                for p, slab in enumerate(row_pairs):
                    width = min(pw, nk - p * pw)
                    pieces.append(s[rr * GRID_W:(rr + 1) * GRID_W, p * pw:p * pw + width]
                                  + bank_ref[0, slab, :, :width])
                t_rows.append(jnp.concatenate(pieces, axis=1))
            t = jnp.concatenate(t_rows, axis=0)
            m = jnp.max(t, axis=-1, keepdims=True)
            e = jnp.exp2(t - m).astype(jnp.bfloat16)
            ol = jnp.dot(e, _with_ones(vb), preferred_element_type=jnp.float32)
            o_ref[bb, g * nq:(g + 1) * nq, :] = (ol[:, :HEAD_DIM] / ol[:, HEAD_DIM:]).astype(o_ref.dtype)
        return carry

    lax.fori_loop(0, q_ref.shape[0], one, 0)


def _attn_c(p3, tbl, starts, slabs):
    b, s, _ = p3.shape
    nb = math.gcd(b, ATTN_BATCH)
    return pl.pallas_call(
        functools.partial(_attn_c_kernel, starts=starts, slabs=slabs),
        grid=(C_HEADS, b // nb),
        in_specs=[
            pl.BlockSpec((nb, s, HEAD_DIM), lambda h, bi: (bi, 0, h)),
            pl.BlockSpec((nb, s, HEAD_DIM), lambda h, bi: (bi, 0, C_HEADS + h)),
            pl.BlockSpec((nb, s, HEAD_DIM), lambda h, bi: (bi, 0, 2 * C_HEADS + h)),
            pl.BlockSpec((1,) + tbl.shape[1:], lambda h, bi: (h, 0, 0, 0)),
        ],
        out_specs=pl.BlockSpec((nb, s, HEAD_DIM), lambda h, bi: (bi, 0, h)),
        out_shape=jax.ShapeDtypeStruct((b, s, C_HEADS * HEAD_DIM), jnp.bfloat16),
        compiler_params=_cparams(("parallel", "parallel")),
        name="attn_c",
    )(p3, p3, p3, tbl)


def _gate_out_kernel(*refs, n_y, n_z, final_norm):
    y_refs = refs[:n_y]
    z_refs = refs[n_y:n_y + n_z]
    w_ref, x_ref = refs[n_y + n_z:n_y + n_z + 2]
    o_ref = refs[-1]
    y = jnp.concatenate([r[...] for r in y_refs], axis=1)
    z = jnp.concatenate([r[...] for r in z_refs], axis=1)
    gated = y * (z / (1.0 + jnp.exp(-z)))
    out = x_ref[...] + jnp.dot(gated, w_ref[...], preferred_element_type=jnp.float32)
    if final_norm:
        g_ref = refs[-2]
        ms = jnp.mean(out * out, axis=-1, keepdims=True)
        out = out * lax.rsqrt(ms + RMS_EPS) * g_ref[...]
    o_ref[...] = out


def _gate_out(ys, p2d, z_col, z_blocks, w, x2d, g_final, *, tm, name):
    m, d = x2d.shape
    zw = d // z_blocks
    zb0 = z_col * HEAD_DIM // zw
    assert zb0 * zw == z_col * HEAD_DIM
    in_specs = [pl.BlockSpec((tm, y.shape[1]), lambda i: (i, 0)) for y in ys]
    in_specs += [pl.BlockSpec((tm, zw), functools.partial(lambda i, c: (i, c), c=zb0 + c)) for c in range(z_blocks)]
    in_specs += [pl.BlockSpec(w.shape, lambda i: (0, 0)), pl.BlockSpec((tm, d), lambda i: (i, 0))]
    args = list(ys) + [p2d] * z_blocks + [w, x2d]
    if g_final is not None:
        in_specs.append(pl.BlockSpec((1, d), lambda i: (0, 0)))
        args.append(g_final.reshape(1, d))
    return pl.pallas_call(
        functools.partial(_gate_out_kernel, n_y=len(ys), n_z=z_blocks, final_norm=g_final is not None),
        grid=(m // tm,),
        in_specs=in_specs,
        out_specs=pl.BlockSpec((tm, d), lambda i: (i, 0)),
        out_shape=jax.ShapeDtypeStruct((m, d), jnp.float32),
        compiler_params=_cparams(("parallel",)),
        name=name,
    )(*args)


def kernel(x, ln_ab, w_in_ab, sink_a, w_out_ab, ln_c, w_in_c, rpb_c, w_out_c, ln_f):
    b, s, d = x.shape
    assert s == 2048 and d == 2048, "tiling below is written for SEQ = D_MODEL = 2048"
    assert ln_ab.shape[0] == 1 and ln_c.shape[0] == 1, "depth-2 trunk: one layer of each kind"
    m = b * s
    bf16 = jnp.bfloat16
    x2d = x.reshape(m, d)

    slopes = _alibi_slopes(A_Q_HEADS + B_HEADS)
    aw = A_HALF_WINDOW
    tbl_a = jnp.stack([_band_bias(aw, 3 * aw, off, aw, 1, slopes[:A_Q_HEADS]) for off in (0, aw, 2 * aw)],
                      axis=1)
    qb, kw = 2 * B_HALF, 4 * B_HALF
    tbl_b = jnp.stack(
        [_band_bias(qb, kw, off, B_HALF, dil, slopes[A_Q_HEADS:]) for dil in (1, 4) for off in (0, B_HALF, 2 * B_HALF)]
        + [_band_bias(qb, kw, 0, B_HALF, B_CONFIGS[2][1], slopes[A_Q_HEADS:])], axis=1)
    na_starts, na_slabs, na_pairs = _na_plan(s // GRID_W)
    tbl_c = _na_bias_bank(rpb_c[0], na_pairs)

    hd = HEAD_DIM
    q_cols_ab = ((A_Q_COL * hd, A_K_COL * hd), (B_Q_COL * hd, B_K_COL * hd))
    p = _norm_proj(x2d, ln_ab[0], w_in_ab[0].astype(bf16), q_cols_ab, tm=512, name="norm_proj_ab")
    p3 = p.reshape(b, s, p.shape[1])
    ya = _attn_a(p3, sink_a[0], tbl_a)
    yb = _attn_b(p3, tbl_b)
    x1 = _gate_out([ya.reshape(m, -1), yb.reshape(m, -1)], p, Z_AB_COL, 4, w_out_ab[0].astype(bf16), x2d, None,
                   tm=512, name="gate_out_ab")

    pc = _norm_proj(x1, ln_c[0], w_in_c[0].astype(bf16), ((0, C_HEADS * hd),), tm=512, name="norm_proj_c")
    yc = _attn_c(pc.reshape(b, s, pc.shape[1]), tbl_c, na_starts, na_slabs)
    out = _gate_out([yc.reshape(m, -1)], pc, 3 * C_HEADS, 1, w_out_c[0].astype(bf16), x1, ln_f,
                    tm=512, name="gate_out_c")
    return out.reshape(b, s, d)
```

```python
import functools
import math

import numpy as np
import jax
import jax.numpy as jnp
from jax import lax
from jax.experimental import pallas as pl
from jax.experimental.pallas import tpu as pltpu

HEAD_DIM = 128
A_Q_HEADS = 8
A_KV_HEADS = 2
A_GROUP = A_Q_HEADS // A_KV_HEADS
A_HALF_WINDOW = 128
B_HEADS = 8
B_CONFIGS = ((128, 1), (512, 4), (2048, 16))
B_HALF = 64
C_HEADS = 16
GRID_W = 64
NA_KH = 8
NA_KW = 16
RMS_EPS = 1e-5
NEG = -1e30
LOG2E = 1.4426950408889634
QK_SCALE = HEAD_DIM ** -0.5 * LOG2E

A_Q_COL = 0
A_K_COL = A_Q_HEADS
A_V_COL = A_K_COL + A_KV_HEADS
B_Q_COL = A_V_COL + A_KV_HEADS
B_K_COL = B_Q_COL + B_HEADS
B_V_COL = B_K_COL + B_HEADS

NA_QROWS = 2
NA_KROWS = NA_QROWS + NA_KH - 1

A_UNROLL = 8
B_UNROLL = 16
ATTN_BATCH = 4

VMEM_LIMIT = 60 * 1024 * 1024

_NT = (((1,), (1,)), ((), ()))


def _cparams(sem):
    return pltpu.CompilerParams(dimension_semantics=sem, vmem_limit_bytes=VMEM_LIMIT)


def _edge_variant(i, n):
    return jnp.where(i == 0, 0, jnp.where(i == n - 1, 2, 1))


def _with_ones(vb):
    return jnp.concatenate([vb, jnp.ones_like(vb)], axis=1)


def _norm_proj_kernel(x_ref, g_ref, w_ref, cs_ref, qkv_ref, z_ref):
    x = x_ref[...]
    ms = jnp.mean(x * x, axis=-1, keepdims=True)
    hn = (x * lax.rsqrt(ms + RMS_EPS) * g_ref[...]).astype(jnp.bfloat16)
    acc = jnp.dot(hn, w_ref[...], preferred_element_type=jnp.float32)
    res = (acc * cs_ref[...]).astype(qkv_ref.dtype)
    n_qkv = qkv_ref.shape[1]
    qkv_ref[...] = res[:, :n_qkv]
    z_ref[...] = res[:, n_qkv:]


def _norm_proj(x2d, g, w, q_cols, *, tm, name):
    m, d = x2d.shape
    n = w.shape[1]
    cs = np.ones((1, n), np.float32)
    for lo, hi in q_cols:
        cs[:, lo:hi] = QK_SCALE
    return pl.pallas_call(
        _norm_proj_kernel,
        grid=(m // tm,),
        in_specs=[
            pl.BlockSpec((tm, d), lambda i: (i, 0)),
            pl.BlockSpec((1, d), lambda i: (0, 0)),
            pl.BlockSpec((d, n), lambda i: (0, 0), pipeline_mode=pl.Buffered(1)),
            pl.BlockSpec((1, n), lambda i: (0, 0)),
        ],
        out_specs=[pl.BlockSpec((tm, n - d), lambda i: (i, 0)), pl.BlockSpec((tm, d), lambda i: (i, 0))],
        out_shape=[jax.ShapeDtypeStruct((m, n - d), jnp.bfloat16), jax.ShapeDtypeStruct((m, d), jnp.bfloat16)],
        compiler_params=_cparams(("parallel",)),
        name=name,
    )(x2d, g.reshape(1, d), w, jnp.asarray(cs))


def _alibi_slopes(n):
    return jnp.exp2(-8.0 * jnp.arange(1, n + 1, dtype=jnp.float32) / n)


def _band_bias(nq, nk, off, half, dist_scale, slopes):
    rel = np.arange(nk)[None, :] - np.arange(nq)[:, None] - off
    valid = np.abs(rel) <= half
    dist = jnp.asarray((np.abs(rel) * dist_scale).astype(np.float32))
    bias = -slopes[:, None, None] * dist
    return jnp.where(valid[None], bias * LOG2E, NEG)


def _na_plan(rows):
    masked = 2 * NA_KH - 1
    starts, slabs, pairs = [], [], []
    for g in range(rows // NA_QROWS):
        ws = int(np.clip(g * NA_QROWS - NA_KH // 2, 0, rows - NA_KROWS))
        per_row = []
        for rr in range(NA_QROWS):
            r = g * NA_QROWS + rr
            r0 = int(np.clip(r - NA_KH // 2, 0, rows - NA_KH))
            assert ws <= r0 and r0 + NA_KH <= ws + NA_KROWS
            ids = [kr - r + NA_KH - 1 if r0 <= kr < r0 + NA_KH else masked for kr in range(ws, ws + NA_KROWS)]
            ids += [masked] * (len(ids) % 2)
            row_pairs = []
            for p in range(0, len(ids), 2):
                pair = (ids[p], ids[p + 1])
                if pair not in pairs:
                    pairs.append(pair)
                row_pairs.append(pairs.index(pair))
            per_row.append(tuple(row_pairs))
        starts.append(ws)
        slabs.append(tuple(per_row))
    return tuple(starts), tuple(slabs), tuple(pairs)


def _na_bias_bank(rpb, pairs):
    h, n_dr, n_dc = rpb.shape
    cq = np.arange(GRID_W)
    c0 = np.clip(cq - NA_KW // 2, 0, GRID_W - NA_KW)
    ck = np.arange(GRID_W)
    col_valid = (ck[None, :] >= c0[:, None]) & (ck[None, :] < c0[:, None] + NA_KW)
    dc = ck[None, :] - cq[:, None] + NA_KW - 1
    onehot = (dc[None] == np.arange(n_dc)[:, None, None]) & col_valid[None]
    rhs = np.zeros((2, n_dc, GRID_W, 2, GRID_W), np.float32)
    for j in range(2):
        rhs[j, :, :, j, :] = onehot
    sel = np.asarray(pairs, np.int32)
    rp = jnp.concatenate([rpb * LOG2E, jnp.zeros((h, 1, n_dc), jnp.float32)], axis=1)
    lhs = jnp.take(rp, jnp.asarray(sel.reshape(-1)), axis=1).reshape(h, len(pairs), 2 * n_dc)
    bank = jnp.einsum("hnx,xqk->hnqk", lhs, jnp.asarray(rhs.reshape(2 * n_dc, GRID_W, 2 * GRID_W)),
                      precision=lax.Precision.HIGHEST)
    valid = (sel != n_dr)[:, None, :, None] & col_valid[None, :, None, :]
    return jnp.where(jnp.asarray(valid.reshape(len(pairs), GRID_W, 2 * GRID_W))[None], bank, NEG)


def _attn_a_kernel(sink_ref, q_ref, k_ref, v_ref, tbl_ref, o_ref, *, seq):
    blk = A_HALF_WINDOW
    nk = 3 * blk
    nblk = seq // blk
    g = pl.program_id(0)
    sinks = [sink_ref[g * A_GROUP + j] * LOG2E for j in range(A_GROUP)]

    def block(bb, i):
        qrow = pl.multiple_of(i * blk, blk)
        krow = pl.multiple_of(jnp.clip((i - 1) * blk, 0, seq - nk), blk)
        var = _edge_variant(i, nblk)
        q4 = q_ref[bb, pl.ds(qrow, blk), :]
        qs = jnp.concatenate([q4[:, j * HEAD_DIM:(j + 1) * HEAD_DIM] for j in range(A_GROUP)], axis=0)
        kb = k_ref[bb, pl.ds(krow, nk), :]
        vb = v_ref[bb, pl.ds(krow, nk), :]
        s = lax.dot_general(qs, kb, _NT, preferred_element_type=jnp.float32)
        es, ls = [], []
        for j in range(A_GROUP):
            t = s[j * blk:(j + 1) * blk, :] + tbl_ref[j, var]
            m = jnp.maximum(jnp.max(t, axis=-1, keepdims=True), sinks[j])
            e = jnp.exp2(t - m)
            ls.append(jnp.sum(e, axis=-1, keepdims=True) + jnp.exp2(sinks[j] - m))
            es.append(e.astype(jnp.bfloat16))
        o = jnp.dot(jnp.concatenate(es, axis=0), vb, preferred_element_type=jnp.float32)
        out = jnp.concatenate([o[j * blk:(j + 1) * blk] / ls[j] for j in range(A_GROUP)], axis=1)
        o_ref[bb, pl.ds(qrow, blk), :] = out.astype(o_ref.dtype)

    iters = nblk // A_UNROLL

    def body(n, carry):
        bb, it = n // iters, n % iters
        for u in range(A_UNROLL):
            block(bb, it * A_UNROLL + u)
        return carry

    lax.fori_loop(0, q_ref.shape[0] * iters, body, 0)


def _attn_a(p3, sink, tbl):
    b, s, _ = p3.shape
    gw = A_GROUP * HEAD_DIM
    nb = math.gcd(b, ATTN_BATCH)
    return pl.pallas_call(
        functools.partial(_attn_a_kernel, seq=s),
        grid=(A_KV_HEADS, b // nb),
        in_specs=[
            pl.BlockSpec(memory_space=pltpu.SMEM),
            pl.BlockSpec((nb, s, gw), lambda g, bi: (bi, 0, g)),
            pl.BlockSpec((nb, s, HEAD_DIM), lambda g, bi: (bi, 0, A_K_COL + g)),
            pl.BlockSpec((nb, s, HEAD_DIM), lambda g, bi: (bi, 0, A_V_COL + g)),
            pl.BlockSpec((A_GROUP,) + tbl.shape[1:], lambda g, bi: (g, 0, 0, 0)),
        ],
        out_specs=pl.BlockSpec((nb, s, gw), lambda g, bi: (bi, 0, g)),
        out_shape=jax.ShapeDtypeStruct((b, s, A_Q_HEADS * HEAD_DIM), jnp.bfloat16),
        compiler_params=_cparams(("parallel", "parallel")),
        name="attn_a",
    )(sink, p3, p3, p3, tbl)


def _attn_b_kernel(*refs, seq):
    batch_rows = refs[0].shape[0]
    lax.fori_loop(0, batch_rows, functools.partial(_attn_b_one, refs=refs, seq=seq), 0)


def _attn_b_one(bb, carry, *, refs, seq):
    (q_ref, k_ref, v_ref, tbl_ref, o_ref, qf, kf, vf, q4, k4, v4, of,
     acc0, acc1, acc2, m0, m1, m2, l0, l1, l2) = refs
    qb_rows = 2 * B_HALF
    kw = 4 * B_HALF
    fold = B_CONFIGS[1][1]

    def attend(qb, kb, vb, tb):
        s = lax.dot_general(qb, kb, _NT, preferred_element_type=jnp.float32)
        t = s + tb
        m = jnp.max(t, axis=-1, keepdims=True)
        e = jnp.exp2(t - m).astype(jnp.bfloat16)
        al = jnp.dot(e, _with_ones(vb), preferred_element_type=jnp.float32)
        return al[:, :HEAD_DIM], jnp.broadcast_to(m, (qb.shape[0], HEAD_DIM)), al[:, HEAD_DIM:]

    nblk = seq // qb_rows

    def nat_block(i):
        qrow = pl.multiple_of(i * qb_rows, qb_rows)
        krow = pl.multiple_of(jnp.clip(i * qb_rows - B_HALF, 0, seq - kw), B_HALF)
        a, m, l = attend(q_ref[bb, pl.ds(qrow, qb_rows), :], k_ref[bb, pl.ds(krow, kw), :],
                         v_ref[bb, pl.ds(krow, kw), :], tbl_ref[0, _edge_variant(i, nblk)])
        acc0[pl.ds(qrow, qb_rows), :] = a
        m0[pl.ds(qrow, qb_rows), :] = m
        l0[pl.ds(qrow, qb_rows), :] = l

    def nat_body(it, carry):
        for u in range(B_UNROLL):
            nat_block(it * B_UNROLL + u)
        return carry

    lax.fori_loop(0, nblk // B_UNROLL, nat_body, 0)

    qf[...] = q_ref[bb].astype(jnp.float32)
    kf[...] = k_ref[bb].astype(jnp.float32)
    vf[...] = v_ref[bb].astype(jnp.float32)

    def dilated(d, var0, acc, mm, ll):
        ln = seq // d
        nb = ln // qb_rows
        per_iter = max(B_UNROLL // nb, 1)

        def residue(r):
            if d == fold:
                cls = [src[pl.ds(r, ln, stride=d), :] for src in (qf, kf, vf)]
                for dst, val in zip((q4, k4, v4), cls):
                    dst[pl.ds(pl.multiple_of(r * ln, ln), ln), :] = val
            else:
                first = (r % fold) * (seq // fold) + r // fold
                cls = [src[pl.ds(first, ln, stride=d // fold), :] for src in (q4, k4, v4)]
            qr, kr, vr = (val.astype(jnp.bfloat16) for val in cls)
            for i in range(nb):
                if nb == 1:
                    k0, width, var = 0, ln, var0
                else:
                    k0 = min(max(i * qb_rows - B_HALF, 0), ln - kw)
                    width = kw
                    var = var0 + (0 if i == 0 else (2 if i == nb - 1 else 1))
                a, m, l = attend(qr[i * qb_rows:(i + 1) * qb_rows], kr[k0:k0 + width], vr[k0:k0 + width],
                                 tbl_ref[0, var, :, :width])
                start = (r % fold) * (seq // fold) + r // fold + (d // fold) * qb_rows * i
                if d == fold:
                    rows = pl.ds(pl.multiple_of(start, qb_rows), qb_rows)
                else:
                    rows = pl.ds(start, qb_rows, stride=d // fold)
                acc[rows, :] = a
                mm[rows, :] = m
                ll[rows, :] = l

        def body(it, carry):
            for u in range(per_iter):
                residue(it * per_iter + u)
            return carry

        lax.fori_loop(0, d // per_iter, body, 0)

    dilated(B_CONFIGS[1][1], 3, acc1, m1, l1)
    dilated(B_CONFIGS[2][1], 6, acc2, m2, l2)

    merge_rows = 2 * qb_rows
    chunks_per_class = seq // fold // merge_rows

    def merge(c, carry):
        rows = pl.ds(pl.multiple_of(c * merge_rows, merge_rows), merge_rows)
        nat = pl.ds(c // chunks_per_class + fold * merge_rows * (c % chunks_per_class), merge_rows, stride=fold)
        ma, mb, mc = m0[nat, :], m1[rows, :], m2[rows, :]
        mx = jnp.maximum(jnp.maximum(ma, mb), mc)
        wa, wb, wc = jnp.exp2(ma - mx), jnp.exp2(mb - mx), jnp.exp2(mc - mx)
        num = wa * acc0[nat, :] + wb * acc1[rows, :] + wc * acc2[rows, :]
        den = wa * l0[nat, :] + wb * l1[rows, :] + wc * l2[rows, :]
        of[nat, :] = num / den
        return carry

    lax.fori_loop(0, seq // merge_rows, merge, 0)
    o_ref[bb] = of[...].astype(o_ref.dtype)
    return carry


def _attn_b(p3, tbl):
    b, s, _ = p3.shape
    f32_slab = pltpu.VMEM((s, HEAD_DIM), jnp.float32)
    nb = math.gcd(b, ATTN_BATCH)
    return pl.pallas_call(
        functools.partial(_attn_b_kernel, seq=s),
        grid=(B_HEADS, b // nb),
        in_specs=[
            pl.BlockSpec((nb, s, HEAD_DIM), lambda h, bi: (bi, 0, B_Q_COL + h)),
            pl.BlockSpec((nb, s, HEAD_DIM), lambda h, bi: (bi, 0, B_K_COL + h)),
            pl.BlockSpec((nb, s, HEAD_DIM), lambda h, bi: (bi, 0, B_V_COL + h)),
            pl.BlockSpec((1,) + tbl.shape[1:], lambda h, bi: (h, 0, 0, 0)),
        ],
        out_specs=pl.BlockSpec((nb, s, HEAD_DIM), lambda h, bi: (bi, 0, h)),
        out_shape=jax.ShapeDtypeStruct((b, s, B_HEADS * HEAD_DIM), jnp.bfloat16),
        scratch_shapes=[f32_slab] * 16,
        compiler_params=_cparams(("parallel", "parallel")),
        name="attn_b",
    )(p3, p3, p3, tbl)


def _attn_c_kernel(q_ref, k_ref, v_ref, bank_ref, o_ref, *, starts, slabs):
    nq = NA_QROWS * GRID_W
    nk = NA_KROWS * GRID_W
    pw = 2 * GRID_W

    def one(bb, carry):
        for g, (ws, per_row) in enumerate(zip(starts, slabs)):
            qb = q_ref[bb, g * nq:(g + 1) * nq, :]
            kb = k_ref[bb, ws * GRID_W:ws * GRID_W + nk, :]
            vb = v_ref[bb, ws * GRID_W:ws * GRID_W + nk, :]
            s = lax.dot_general(qb, kb, _NT, preferred_element_type=jnp.float32)
            t_rows = []
            for rr, row_pairs in enumerate(per_row):
                pieces = []
                for p, slab in enumerate(row_pairs):
                    width = min(pw, nk - p * pw)
                    pieces.append(s[rr * GRID_W:(rr + 1) * GRID_W, p * pw:p * pw + width]
                                  + bank_ref[0, slab, :, :width])
                t_rows.append(jnp.concatenate(pieces, axis=1))
            t = jnp.concatenate(t_rows, axis=0)
            m = jnp.max(t, axis=-1, keepdims=True)
            e = jnp.exp2(t - m).astype(jnp.bfloat16)
            ol = jnp.dot(e, _with_ones(vb), preferred_element_type=jnp.float32)
            o_ref[bb, g * nq:(g + 1) * nq, :] = (ol[:, :HEAD_DIM] / ol[:, HEAD_DIM:]).astype(o_ref.dtype)
        return carry

    lax.fori_loop(0, q_ref.shape[0], one, 0)


def _attn_c(p3, tbl, starts, slabs):
    b, s, _ = p3.shape
    nb = math.gcd(b, ATTN_BATCH)
    return pl.pallas_call(
        functools.partial(_attn_c_kernel, starts=starts, slabs=slabs),
        grid=(C_HEADS, b // nb),
        in_specs=[
            pl.BlockSpec((nb, s, HEAD_DIM), lambda h, bi: (bi, 0, h)),
            pl.BlockSpec((nb, s, HEAD_DIM), lambda h, bi: (bi, 0, C_HEADS + h)),
            pl.BlockSpec((nb, s, HEAD_DIM), lambda h, bi: (bi, 0, 2 * C_HEADS + h)),
            pl.BlockSpec((1,) + tbl.shape[1:], lambda h, bi: (h, 0, 0, 0)),
        ],
        out_specs=pl.BlockSpec((nb, s, HEAD_DIM), lambda h, bi: (bi, 0, h)),
        out_shape=jax.ShapeDtypeStruct((b, s, C_HEADS * HEAD_DIM), jnp.bfloat16),
        compiler_params=_cparams(("parallel", "parallel")),
        name="attn_c",
    )(p3, p3, p3, tbl)


def _gate_out_kernel(*refs, n_y, final_norm):
    y_refs = refs[:n_y]
    z_ref, w_ref, x_ref = refs[n_y:n_y + 3]
    o_ref = refs[-1]
    y = jnp.concatenate([r[...] for r in y_refs], axis=1)
    z = z_ref[...]
    gated = y * (z / (1.0 + jnp.exp(-z)))
    out = x_ref[...] + jnp.dot(gated, w_ref[...], preferred_element_type=jnp.float32)
    if final_norm:
        g_ref = refs[-2]
        ms = jnp.mean(out * out, axis=-1, keepdims=True)
        out = out * lax.rsqrt(ms + RMS_EPS) * g_ref[...]
    o_ref[...] = out


def _gate_out(ys, z2d, w, x2d, g_final, *, tm, name):
    m, d = x2d.shape
    in_specs = [pl.BlockSpec((tm, y.shape[1]), lambda i: (i, 0)) for y in ys]
    in_specs += [pl.BlockSpec((tm, d), lambda i: (i, 0)), pl.BlockSpec(w.shape, lambda i: (0, 0)),
                 pl.BlockSpec((tm, d), lambda i: (i, 0))]
    args = list(ys) + [z2d, w, x2d]
    if g_final is not None:
        in_specs.append(pl.BlockSpec((1, d), lambda i: (0, 0)))
        args.append(g_final.reshape(1, d))
    return pl.pallas_call(
        functools.partial(_gate_out_kernel, n_y=len(ys), final_norm=g_final is not None),
        grid=(m // tm,),
        in_specs=in_specs,
        out_specs=pl.BlockSpec((tm, d), lambda i: (i, 0)),
        out_shape=jax.ShapeDtypeStruct((m, d), jnp.float32),
        compiler_params=_cparams(("parallel",)),
        name=name,
    )(*args)


def kernel(x, ln_ab, w_in_ab, sink_a, w_out_ab, ln_c, w_in_c, rpb_c, w_out_c, ln_f):
    b, s, d = x.shape
    assert s == 2048 and d == 2048, "tiling below is written for SEQ = D_MODEL = 2048"
    assert ln_ab.shape[0] == 1 and ln_c.shape[0] == 1, "depth-2 trunk: one layer of each kind"
    m = b * s
    bf16 = jnp.bfloat16
    x2d = x.reshape(m, d)

    slopes = _alibi_slopes(A_Q_HEADS + B_HEADS)
    aw = A_HALF_WINDOW
    tbl_a = jnp.stack([_band_bias(aw, 3 * aw, off, aw, 1, slopes[:A_Q_HEADS]) for off in (0, aw, 2 * aw)],
                      axis=1)
    qb, kw = 2 * B_HALF, 4 * B_HALF
    tbl_b = jnp.stack(
        [_band_bias(qb, kw, off, B_HALF, dil, slopes[A_Q_HEADS:]) for dil in (1, 4) for off in (0, B_HALF, 2 * B_HALF)]
        + [_band_bias(qb, kw, 0, B_HALF, B_CONFIGS[2][1], slopes[A_Q_HEADS:])], axis=1)
    na_starts, na_slabs, na_pairs = _na_plan(s // GRID_W)
    tbl_c = _na_bias_bank(rpb_c[0], na_pairs)

    hd = HEAD_DIM
    q_cols_ab = ((A_Q_COL * hd, A_K_COL * hd), (B_Q_COL * hd, B_K_COL * hd))
    p, z_ab = _norm_proj(x2d, ln_ab[0], w_in_ab[0].astype(bf16), q_cols_ab, tm=512, name="norm_proj_ab")
    p3 = p.reshape(b, s, p.shape[1])
    ya = _attn_a(p3, sink_a[0], tbl_a)
    yb = _attn_b(p3, tbl_b)
    x1 = _gate_out([ya.reshape(m, -1), yb.reshape(m, -1)], z_ab, w_out_ab[0].astype(bf16), x2d, None,
                   tm=512, name="gate_out_ab")

    pc, z_c = _norm_proj(x1, ln_c[0], w_in_c[0].astype(bf16), ((0, C_HEADS * hd),), tm=512, name="norm_proj_c")
    yc = _attn_c(pc.reshape(b, s, pc.shape[1]), tbl_c, na_starts, na_slabs)
    out = _gate_out([yc.reshape(m, -1)], z_c, w_out_c[0].astype(bf16), x1, ln_f, tm=512, name="gate_out_c")
    return out.reshape(b, s, d)
```

```python
import functools
import math

import numpy as np
import jax
import jax.numpy as jnp
from jax import lax
from jax.experimental import pallas as pl
from jax.experimental.pallas import tpu as pltpu

HEAD_DIM = 128
A_Q_HEADS = 8
A_KV_HEADS = 2
A_GROUP = A_Q_HEADS // A_KV_HEADS
A_HALF_WINDOW = 128
B_HEADS = 8
B_CONFIGS = ((128, 1), (512, 4), (2048, 16))
B_HALF = 64
C_HEADS = 16
GRID_W = 64
NA_KH = 8
NA_KW = 16
RMS_EPS = 1e-5
NEG = -1e30
LOG2E = 1.4426950408889634
QK_SCALE = HEAD_DIM ** -0.5 * LOG2E

A_Q_COL = 0
A_K_COL = A_Q_HEADS
A_V_COL = A_K_COL + A_KV_HEADS
B_Q_COL = A_V_COL + A_KV_HEADS
B_K_COL = B_Q_COL + B_HEADS
B_V_COL = B_K_COL + B_HEADS

NA_QROWS = 2
NA_KROWS = NA_QROWS + NA_KH - 1

A_UNROLL = 16
B_UNROLL = 16
ATTN_BATCH = 4

VMEM_LIMIT = 60 * 1024 * 1024

_NT = (((1,), (1,)), ((), ()))


def _cparams(sem):
    return pltpu.CompilerParams(dimension_semantics=sem, vmem_limit_bytes=VMEM_LIMIT)


def _edge_variant(i, n):
    return jnp.where(i == 0, 0, jnp.where(i == n - 1, 2, 1))


def _with_ones(vb):
    return jnp.concatenate([vb, jnp.ones_like(vb)], axis=1)


def _norm_proj_kernel(x_ref, g_ref, w_ref, cs_ref, qkv_ref, z_ref):
    x = x_ref[...]
    ms = jnp.mean(x * x, axis=-1, keepdims=True)
    hn = (x * lax.rsqrt(ms + RMS_EPS) * g_ref[...]).astype(jnp.bfloat16)
    acc = jnp.dot(hn, w_ref[...], preferred_element_type=jnp.float32)
    res = (acc * cs_ref[...]).astype(qkv_ref.dtype)
    n_qkv = qkv_ref.shape[1]
    qkv_ref[...] = res[:, :n_qkv]
    z_ref[...] = res[:, n_qkv:]


def _norm_proj(x2d, g, w, q_cols, *, tm, name):
    m, d = x2d.shape
    n = w.shape[1]
    cs = np.ones((1, n), np.float32)
    for lo, hi in q_cols:
        cs[:, lo:hi] = QK_SCALE
    return pl.pallas_call(
        _norm_proj_kernel,
        grid=(m // tm,),
        in_specs=[
            pl.BlockSpec((tm, d), lambda i: (i, 0)),
            pl.BlockSpec((1, d), lambda i: (0, 0)),
            pl.BlockSpec((d, n), lambda i: (0, 0), pipeline_mode=pl.Buffered(1)),
            pl.BlockSpec((1, n), lambda i: (0, 0)),
        ],
        out_specs=[pl.BlockSpec((tm, n - d), lambda i: (i, 0)), pl.BlockSpec((tm, d), lambda i: (i, 0))],
        out_shape=[jax.ShapeDtypeStruct((m, n - d), jnp.bfloat16), jax.ShapeDtypeStruct((m, d), jnp.bfloat16)],
        compiler_params=_cparams(("parallel",)),
        name=name,
    )(x2d, g.reshape(1, d), w, jnp.asarray(cs))


def _alibi_slopes(n):
    return jnp.exp2(-8.0 * jnp.arange(1, n + 1, dtype=jnp.float32) / n)


def _band_bias(nq, nk, off, half, dist_scale, slopes):
    rel = np.arange(nk)[None, :] - np.arange(nq)[:, None] - off
    valid = np.abs(rel) <= half
    dist = jnp.asarray((np.abs(rel) * dist_scale).astype(np.float32))
    bias = -slopes[:, None, None] * dist
    return jnp.where(valid[None], bias * LOG2E, NEG)


def _na_plan(rows):
    masked = 2 * NA_KH - 1
    starts, slabs, pairs = [], [], []
    for g in range(rows // NA_QROWS):
        ws = int(np.clip(g * NA_QROWS - NA_KH // 2, 0, rows - NA_KROWS))
        per_row = []
        for rr in range(NA_QROWS):
            r = g * NA_QROWS + rr
            r0 = int(np.clip(r - NA_KH // 2, 0, rows - NA_KH))
            assert ws <= r0 and r0 + NA_KH <= ws + NA_KROWS
            ids = [kr - r + NA_KH - 1 if r0 <= kr < r0 + NA_KH else masked for kr in range(ws, ws + NA_KROWS)]
            ids += [masked] * (len(ids) % 2)
            row_pairs = []
            for p in range(0, len(ids), 2):
                pair = (ids[p], ids[p + 1])
                if pair not in pairs:
                    pairs.append(pair)
                row_pairs.append(pairs.index(pair))
            per_row.append(tuple(row_pairs))
        starts.append(ws)
        slabs.append(tuple(per_row))
    return tuple(starts), tuple(slabs), tuple(pairs)


def _na_bias_bank(rpb, pairs):
    h, n_dr, n_dc = rpb.shape
    cq = np.arange(GRID_W)
    c0 = np.clip(cq - NA_KW // 2, 0, GRID_W - NA_KW)
    ck = np.arange(GRID_W)
    col_valid = (ck[None, :] >= c0[:, None]) & (ck[None, :] < c0[:, None] + NA_KW)
    dc = ck[None, :] - cq[:, None] + NA_KW - 1
    onehot = (dc[None] == np.arange(n_dc)[:, None, None]) & col_valid[None]
    rhs = np.zeros((2, n_dc, GRID_W, 2, GRID_W), np.float32)
    for j in range(2):
        rhs[j, :, :, j, :] = onehot
    sel = np.asarray(pairs, np.int32)
    rp = jnp.concatenate([rpb * LOG2E, jnp.zeros((h, 1, n_dc), jnp.float32)], axis=1)
    lhs = jnp.take(rp, jnp.asarray(sel.reshape(-1)), axis=1).reshape(h, len(pairs), 2 * n_dc)
    bank = jnp.einsum("hnx,xqk->hnqk", lhs, jnp.asarray(rhs.reshape(2 * n_dc, GRID_W, 2 * GRID_W)),
                      precision=lax.Precision.HIGHEST)
    valid = (sel != n_dr)[:, None, :, None] & col_valid[None, :, None, :]
    return jnp.where(jnp.asarray(valid.reshape(len(pairs), GRID_W, 2 * GRID_W))[None], bank, NEG)


def _attn_a_kernel(sink_ref, q_ref, k_ref, v_ref, tbl_ref, o_ref, *, seq):
    blk = A_HALF_WINDOW
    nk = 3 * blk
    nblk = seq // blk
    g = pl.program_id(0)
    sinks = [sink_ref[g * A_GROUP + j] * LOG2E for j in range(A_GROUP)]

    def block(bb, i):
        qrow = pl.multiple_of(i * blk, blk)
        krow = pl.multiple_of(jnp.clip((i - 1) * blk, 0, seq - nk), blk)
        var = _edge_variant(i, nblk)
        q4 = q_ref[bb, pl.ds(qrow, blk), :]
        kb = k_ref[bb, pl.ds(krow, nk), :]
        v1 = _with_ones(v_ref[bb, pl.ds(krow, nk), :])
        outs = []
        for j in range(A_GROUP):
            s = lax.dot_general(q4[:, j * HEAD_DIM:(j + 1) * HEAD_DIM], kb, _NT, preferred_element_type=jnp.float32)
            t = s + tbl_ref[j, var]
            m = jnp.maximum(jnp.max(t, axis=-1, keepdims=True), sinks[j])
            e = jnp.exp2(t - m).astype(jnp.bfloat16)
            ol = jnp.dot(e, v1, preferred_element_type=jnp.float32)
            outs.append(ol[:, :HEAD_DIM] / (ol[:, HEAD_DIM:] + jnp.exp2(sinks[j] - m)))
        o_ref[bb, pl.ds(qrow, blk), :] = jnp.concatenate(outs, axis=1).astype(o_ref.dtype)

    iters = nblk // A_UNROLL

    def body(n, carry):
        bb, it = n // iters, n % iters
        for u in range(A_UNROLL):
            block(bb, it * A_UNROLL + u)
        return carry

    lax.fori_loop(0, q_ref.shape[0] * iters, body, 0)


def _attn_a(p3, sink, tbl):
    b, s, _ = p3.shape
    gw = A_GROUP * HEAD_DIM
    nb = math.gcd(b, ATTN_BATCH)
    return pl.pallas_call(
        functools.partial(_attn_a_kernel, seq=s),
        grid=(A_KV_HEADS, b // nb),
        in_specs=[
            pl.BlockSpec(memory_space=pltpu.SMEM),
            pl.BlockSpec((nb, s, gw), lambda g, bi: (bi, 0, g)),
            pl.BlockSpec((nb, s, HEAD_DIM), lambda g, bi: (bi, 0, A_K_COL + g)),
            pl.BlockSpec((nb, s, HEAD_DIM), lambda g, bi: (bi, 0, A_V_COL + g)),
            pl.BlockSpec((A_GROUP,) + tbl.shape[1:], lambda g, bi: (g, 0, 0, 0)),
        ],
        out_specs=pl.BlockSpec((nb, s, gw), lambda g, bi: (bi, 0, g)),
        out_shape=jax.ShapeDtypeStruct((b, s, A_Q_HEADS * HEAD_DIM), jnp.bfloat16),
        compiler_params=_cparams(("parallel", "parallel")),
        name="attn_a",
    )(sink, p3, p3, p3, tbl)


def _attn_b_kernel(*refs, seq):
    batch_rows = refs[0].shape[0]
    lax.fori_loop(0, batch_rows, functools.partial(_attn_b_one, refs=refs, seq=seq), 0)


def _attn_b_one(bb, carry, *, refs, seq):
    (q_ref, k_ref, v_ref, tbl_ref, o_ref, qf, kf, vf, q4, k4, v4, of,
     acc0, acc1, acc2, m0, m1, m2, l0, l1, l2) = refs
    qb_rows = 2 * B_HALF
    kw = 4 * B_HALF
    fold = B_CONFIGS[1][1]

    def attend(qb, kb, vb, tb):
        s = lax.dot_general(qb, kb, _NT, preferred_element_type=jnp.float32)
        t = s + tb
        m = jnp.max(t, axis=-1, keepdims=True)
        e = jnp.exp2(t - m).astype(jnp.bfloat16)
        al = jnp.dot(e, _with_ones(vb), preferred_element_type=jnp.float32)
        return al[:, :HEAD_DIM], jnp.broadcast_to(m, (qb.shape[0], HEAD_DIM)), al[:, HEAD_DIM:]

    nblk = seq // qb_rows

    def nat_block(i):
        qrow = pl.multiple_of(i * qb_rows, qb_rows)
        krow = pl.multiple_of(jnp.clip(i * qb_rows - B_HALF, 0, seq - kw), B_HALF)
        a, m, l = attend(q_ref[bb, pl.ds(qrow, qb_rows), :], k_ref[bb, pl.ds(krow, kw), :],
                         v_ref[bb, pl.ds(krow, kw), :], tbl_ref[0, _edge_variant(i, nblk)])
        acc0[pl.ds(qrow, qb_rows), :] = a
        m0[pl.ds(qrow, qb_rows), :] = m
        l0[pl.ds(qrow, qb_rows), :] = l

    def nat_body(it, carry):
        for u in range(B_UNROLL):
            nat_block(it * B_UNROLL + u)
        return carry

    lax.fori_loop(0, nblk // B_UNROLL, nat_body, 0)

    qf[...] = q_ref[bb].astype(jnp.float32)
    kf[...] = k_ref[bb].astype(jnp.float32)
    vf[...] = v_ref[bb].astype(jnp.float32)

    def dilated(d, var0, acc, mm, ll):
        ln = seq // d
        nb = ln // qb_rows
        per_iter = max(B_UNROLL // nb, 1)

        def residue(r):
            if d == fold:
                cls = [src[pl.ds(r, ln, stride=d), :] for src in (qf, kf, vf)]
                for dst, val in zip((q4, k4, v4), cls):
                    dst[pl.ds(pl.multiple_of(r * ln, ln), ln), :] = val
            else:
                first = (r % fold) * (seq // fold) + r // fold
                cls = [src[pl.ds(first, ln, stride=d // fold), :] for src in (q4, k4, v4)]
            qr, kr, vr = (val.astype(jnp.bfloat16) for val in cls)
            for i in range(nb):
                if nb == 1:
                    k0, width, var = 0, ln, var0
                else:
                    k0 = min(max(i * qb_rows - B_HALF, 0), ln - kw)
                    width = kw
                    var = var0 + (0 if i == 0 else (2 if i == nb - 1 else 1))
                a, m, l = attend(qr[i * qb_rows:(i + 1) * qb_rows], kr[k0:k0 + width], vr[k0:k0 + width],
                                 tbl_ref[0, var, :, :width])
                start = (r % fold) * (seq // fold) + r // fold + (d // fold) * qb_rows * i
                if d == fold:
                    rows = pl.ds(pl.multiple_of(start, qb_rows), qb_rows)
                else:
                    rows = pl.ds(start, qb_rows, stride=d // fold)
                acc[rows, :] = a
                mm[rows, :] = m
                ll[rows, :] = l

        def body(it, carry):
            for u in range(per_iter):
                residue(it * per_iter + u)
            return carry

        lax.fori_loop(0, d // per_iter, body, 0)

    dilated(B_CONFIGS[1][1], 3, acc1, m1, l1)
    dilated(B_CONFIGS[2][1], 6, acc2, m2, l2)

    merge_rows = 2 * qb_rows
    chunks_per_class = seq // fold // merge_rows

    def merge(c, carry):
        rows = pl.ds(pl.multiple_of(c * merge_rows, merge_rows), merge_rows)
        nat = pl.ds(c // chunks_per_class + fold * merge_rows * (c % chunks_per_class), merge_rows, stride=fold)
        ma, mb, mc = m0[nat, :], m1[rows, :], m2[rows, :]
        mx = jnp.maximum(jnp.maximum(ma, mb), mc)
        wa, wb, wc = jnp.exp2(ma - mx), jnp.exp2(mb - mx), jnp.exp2(mc - mx)
        num = wa * acc0[nat, :] + wb * acc1[rows, :] + wc * acc2[rows, :]
        den = wa * l0[nat, :] + wb * l1[rows, :] + wc * l2[rows, :]
        of[nat, :] = num / den
        return carry

    lax.fori_loop(0, seq // merge_rows, merge, 0)
    o_ref[bb] = of[...].astype(o_ref.dtype)
    return carry


def _attn_b(p3, tbl):
    b, s, _ = p3.shape
    f32_slab = pltpu.VMEM((s, HEAD_DIM), jnp.float32)
    nb = math.gcd(b, ATTN_BATCH)
    return pl.pallas_call(
        functools.partial(_attn_b_kernel, seq=s),
        grid=(B_HEADS, b // nb),
        in_specs=[
            pl.BlockSpec((nb, s, HEAD_DIM), lambda h, bi: (bi, 0, B_Q_COL + h)),
            pl.BlockSpec((nb, s, HEAD_DIM), lambda h, bi: (bi, 0, B_K_COL + h)),
            pl.BlockSpec((nb, s, HEAD_DIM), lambda h, bi: (bi, 0, B_V_COL + h)),
            pl.BlockSpec((1,) + tbl.shape[1:], lambda h, bi: (h, 0, 0, 0)),
        ],
        out_specs=pl.BlockSpec((nb, s, HEAD_DIM), lambda h, bi: (bi, 0, h)),
        out_shape=jax.ShapeDtypeStruct((b, s, B_HEADS * HEAD_DIM), jnp.bfloat16),
        scratch_shapes=[f32_slab] * 16,
        compiler_params=_cparams(("parallel", "parallel")),
        name="attn_b",
    )(p3, p3, p3, tbl)


def _attn_c_kernel(q_ref, k_ref, v_ref, bank_ref, o_ref, *, starts, slabs):
    nq = NA_QROWS * GRID_W
    nk = NA_KROWS * GRID_W
    pw = 2 * GRID_W

    def one(bb, carry):
        for g, (ws, per_row) in enumerate(zip(starts, slabs)):
            qb = q_ref[bb, g * nq:(g + 1) * nq, :]
            kb = k_ref[bb, ws * GRID_W:ws * GRID_W + nk, :]
            vb = v_ref[bb, ws * GRID_W:ws * GRID_W + nk, :]
            s = lax.dot_general(qb, kb, _NT, preferred_element_type=jnp.float32)
            t_rows = []
            for rr, row_pairs in enumerate(per_row):
                pieces = []
                for p, slab in enumerate(row_pairs):
                    width = min(pw, nk - p * pw)
                    pieces.append(s[rr * GRID_W:(rr + 1) * GRID_W, p * pw:p * pw + width]
                                  + bank_ref[0, slab, :, :width])
                t_rows.append(jnp.concatenate(pieces, axis=1))
            t = jnp.concatenate(t_rows, axis=0)
            m = jnp.max(t, axis=-1, keepdims=True)
            e = jnp.exp2(t - m).astype(jnp.bfloat16)
            ol = jnp.dot(e, _with_ones(vb), preferred_element_type=jnp.float32)
            o_ref[bb, g * nq:(g + 1) * nq, :] = (ol[:, :HEAD_DIM] / ol[:, HEAD_DIM:]).astype(o_ref.dtype)
        return carry

    lax.fori_loop(0, q_ref.shape[0], one, 0)


def _attn_c(p3, tbl, starts, slabs):
    b, s, _ = p3.shape
    nb = math.gcd(b, ATTN_BATCH)
    return pl.pallas_call(
        functools.partial(_attn_c_kernel, starts=starts, slabs=slabs),
        grid=(C_HEADS, b // nb),
        in_specs=[
            pl.BlockSpec((nb, s, HEAD_DIM), lambda h, bi: (bi, 0, h)),
            pl.BlockSpec((nb, s, HEAD_DIM), lambda h, bi: (bi, 0, C_HEADS + h)),
            pl.BlockSpec((nb, s, HEAD_DIM), lambda h, bi: (bi, 0, 2 * C_HEADS + h)),
            pl.BlockSpec((1,) + tbl.shape[1:], lambda h, bi: (h, 0, 0, 0)),
        ],
        out_specs=pl.BlockSpec((nb, s, HEAD_DIM), lambda h, bi: (bi, 0, h)),
        out_shape=jax.ShapeDtypeStruct((b, s, C_HEADS * HEAD_DIM), jnp.bfloat16),
        compiler_params=_cparams(("parallel", "parallel")),
        name="attn_c",
    )(p3, p3, p3, tbl)


def _gate_out_kernel(*refs, n_y, final_norm):
    y_refs = refs[:n_y]
    z_ref, w_ref, x_ref = refs[n_y:n_y + 3]
    o_ref = refs[-1]
    y = jnp.concatenate([r[...] for r in y_refs], axis=1)
    z = z_ref[...]
    gated = y * (z / (1.0 + jnp.exp(-z)))
    out = x_ref[...] + jnp.dot(gated, w_ref[...], preferred_element_type=jnp.float32)
    if final_norm:
        g_ref = refs[-2]
        ms = jnp.mean(out * out, axis=-1, keepdims=True)
        out = out * lax.rsqrt(ms + RMS_EPS) * g_ref[...]
    o_ref[...] = out


def _gate_out(ys, z2d, w, x2d, g_final, *, tm, name):
    m, d = x2d.shape
    in_specs = [pl.BlockSpec((tm, y.shape[1]), lambda i: (i, 0)) for y in ys]
    in_specs += [pl.BlockSpec((tm, d), lambda i: (i, 0)), pl.BlockSpec(w.shape, lambda i: (0, 0)),
                 pl.BlockSpec((tm, d), lambda i: (i, 0))]
    args = list(ys) + [z2d, w, x2d]
    if g_final is not None:
        in_specs.append(pl.BlockSpec((1, d), lambda i: (0, 0)))
        args.append(g_final.reshape(1, d))
    return pl.pallas_call(
        functools.partial(_gate_out_kernel, n_y=len(ys), final_norm=g_final is not None),
        grid=(m // tm,),
        in_specs=in_specs,
        out_specs=pl.BlockSpec((tm, d), lambda i: (i, 0)),
        out_shape=jax.ShapeDtypeStruct((m, d), jnp.float32),
        compiler_params=_cparams(("parallel",)),
        name=name,
    )(*args)


def kernel(x, ln_ab, w_in_ab, sink_a, w_out_ab, ln_c, w_in_c, rpb_c, w_out_c, ln_f):
    b, s, d = x.shape
    assert s == 2048 and d == 2048, "tiling below is written for SEQ = D_MODEL = 2048"
    assert ln_ab.shape[0] == 1 and ln_c.shape[0] == 1, "depth-2 trunk: one layer of each kind"
    m = b * s
    bf16 = jnp.bfloat16
    x2d = x.reshape(m, d)

    slopes = _alibi_slopes(A_Q_HEADS + B_HEADS)
    aw = A_HALF_WINDOW
    tbl_a = jnp.stack([_band_bias(aw, 3 * aw, off, aw, 1, slopes[:A_Q_HEADS]) for off in (0, aw, 2 * aw)],
                      axis=1)
    qb, kw = 2 * B_HALF, 4 * B_HALF
    tbl_b = jnp.stack(
        [_band_bias(qb, kw, off, B_HALF, dil, slopes[A_Q_HEADS:]) for dil in (1, 4) for off in (0, B_HALF, 2 * B_HALF)]
        + [_band_bias(qb, kw, 0, B_HALF, B_CONFIGS[2][1], slopes[A_Q_HEADS:])], axis=1)
    na_starts, na_slabs, na_pairs = _na_plan(s // GRID_W)
    tbl_c = _na_bias_bank(rpb_c[0], na_pairs)

    hd = HEAD_DIM
    q_cols_ab = ((A_Q_COL * hd, A_K_COL * hd), (B_Q_COL * hd, B_K_COL * hd))
    p, z_ab = _norm_proj(x2d, ln_ab[0], w_in_ab[0].astype(bf16), q_cols_ab, tm=512, name="norm_proj_ab")
    p3 = p.reshape(b, s, p.shape[1])
    ya = _attn_a(p3, sink_a[0], tbl_a)
    yb = _attn_b(p3, tbl_b)
    x1 = _gate_out([ya.reshape(m, -1), yb.reshape(m, -1)], z_ab, w_out_ab[0].astype(bf16), x2d, None,
                   tm=512, name="gate_out_ab")

    pc, z_c = _norm_proj(x1, ln_c[0], w_in_c[0].astype(bf16), ((0, C_HEADS * hd),), tm=512, name="norm_proj_c")
    yc = _attn_c(pc.reshape(b, s, pc.shape[1]), tbl_c, na_starts, na_slabs)
    out = _gate_out([yc.reshape(m, -1)], z_c, w_out_c[0].astype(bf16), x1, ln_f, tm=512, name="gate_out_c")
    return out.reshape(b, s, d)
```

```python
import functools
import math

import numpy as np
import jax
import jax.numpy as jnp
from jax import lax
from jax.experimental import pallas as pl
from jax.experimental.pallas import tpu as pltpu

HEAD_DIM = 128
A_Q_HEADS = 8
A_KV_HEADS = 2
A_GROUP = A_Q_HEADS // A_KV_HEADS
A_HALF_WINDOW = 128
B_HEADS = 8
B_CONFIGS = ((128, 1), (512, 4), (2048, 16))
B_HALF = 64
C_HEADS = 16
GRID_W = 64
NA_KH = 8
NA_KW = 16
RMS_EPS = 1e-5
NEG = float("-inf")
LOG2E = 1.4426950408889634
QK_SCALE = HEAD_DIM ** -0.5 * LOG2E

A_Q_COL = 0
A_K_COL = A_Q_HEADS
A_V_COL = A_K_COL + A_KV_HEADS
B_Q_COL = A_V_COL + A_KV_HEADS
B_K_COL = B_Q_COL + B_HEADS
B_V_COL = B_K_COL + B_HEADS

NA_QROWS = 2
NA_KROWS = NA_QROWS + NA_KH - 1

A_UNROLL = 16
B_UNROLL = 16
ATTN_BATCH = 4

VMEM_LIMIT = 60 * 1024 * 1024

_NT = (((1,), (1,)), ((), ()))


def _cparams(sem):
    return pltpu.CompilerParams(dimension_semantics=sem, vmem_limit_bytes=VMEM_LIMIT)


def _edge_variant(i, n):
    return jnp.where(i == 0, 0, jnp.where(i == n - 1, 2, 1))


def _with_ones(vb):
    return jnp.concatenate([vb, jnp.ones_like(vb)], axis=1)


def _norm_proj_kernel(x_ref, g_ref, w_ref, cs_ref, qkv_ref, z_ref):
    x = x_ref[...]
    ms = jnp.mean(x * x, axis=-1, keepdims=True)
    hn = (x * lax.rsqrt(ms + RMS_EPS) * g_ref[...]).astype(jnp.bfloat16)
    acc = jnp.dot(hn, w_ref[...], preferred_element_type=jnp.float32)
    res = (acc * cs_ref[...]).astype(qkv_ref.dtype)
    n_qkv = qkv_ref.shape[1]
    qkv_ref[...] = res[:, :n_qkv]
    z_ref[...] = res[:, n_qkv:]


def _norm_proj(x2d, g, w, q_cols, *, tm, name):
    m, d = x2d.shape
    n = w.shape[1]
    cs = np.ones((1, n), np.float32)
    for lo, hi in q_cols:
        cs[:, lo:hi] = QK_SCALE
    return pl.pallas_call(
        _norm_proj_kernel,
        grid=(m // tm,),
        in_specs=[
            pl.BlockSpec((tm, d), lambda i: (i, 0)),
            pl.BlockSpec((1, d), lambda i: (0, 0)),
            pl.BlockSpec((d, n), lambda i: (0, 0), pipeline_mode=pl.Buffered(1)),
            pl.BlockSpec((1, n), lambda i: (0, 0)),
        ],
        out_specs=[pl.BlockSpec((tm, n - d), lambda i: (i, 0)), pl.BlockSpec((tm, d), lambda i: (i, 0))],
        out_shape=[jax.ShapeDtypeStruct((m, n - d), jnp.bfloat16), jax.ShapeDtypeStruct((m, d), jnp.bfloat16)],
        compiler_params=_cparams(("parallel",)),
        name=name,
    )(x2d, g.reshape(1, d), w, jnp.asarray(cs))


def _alibi_slopes(n):
    return np.exp2(np.float32(-8.0) * np.arange(1, n + 1, dtype=np.float32) / np.float32(n))


def _band_bias(nq, nk, off, half, dist_scale, slopes):
    rel = np.arange(nk)[None, :] - np.arange(nq)[:, None] - off
    valid = np.abs(rel) <= half
    dist = (np.abs(rel) * dist_scale).astype(np.float32)
    bias = -slopes[:, None, None] * dist
    return np.where(valid[None], bias * np.float32(LOG2E), np.float32(NEG))


def _na_plan(rows):
    masked = 2 * NA_KH - 1
    starts, slabs, pairs = [], [], []
    for g in range(rows // NA_QROWS):
        ws = int(np.clip(g * NA_QROWS - NA_KH // 2, 0, rows - NA_KROWS))
        per_row = []
        for rr in range(NA_QROWS):
            r = g * NA_QROWS + rr
            r0 = int(np.clip(r - NA_KH // 2, 0, rows - NA_KH))
            assert ws <= r0 and r0 + NA_KH <= ws + NA_KROWS
            ids = [kr - r + NA_KH - 1 if r0 <= kr < r0 + NA_KH else masked for kr in range(ws, ws + NA_KROWS)]
            ids += [masked] * (len(ids) % 2)
            row_pairs = []
            for p in range(0, len(ids), 2):
                pair = (ids[p], ids[p + 1])
                if pair not in pairs:
                    pairs.append(pair)
                row_pairs.append(pairs.index(pair))
            per_row.append(tuple(row_pairs))
        starts.append(ws)
        slabs.append(tuple(per_row))
    return tuple(starts), tuple(slabs), tuple(pairs)


def _na_bias_bank(rpb, pairs):
    h, n_dr, n_dc = rpb.shape
    cq = np.arange(GRID_W)
    c0 = np.clip(cq - NA_KW // 2, 0, GRID_W - NA_KW)
    ck = np.arange(GRID_W)
    col_valid = (ck[None, :] >= c0[:, None]) & (ck[None, :] < c0[:, None] + NA_KW)
    dc = ck[None, :] - cq[:, None] + NA_KW - 1
    onehot = (dc[None] == np.arange(n_dc)[:, None, None]) & col_valid[None]
    rhs = np.zeros((2, n_dc, GRID_W, 2, GRID_W), np.float32)
    for j in range(2):
        rhs[j, :, :, j, :] = onehot
    sel = np.asarray(pairs, np.int32)
    rp = jnp.concatenate([rpb * LOG2E, jnp.zeros((h, 1, n_dc), jnp.float32)], axis=1)
    lhs = jnp.take(rp, jnp.asarray(sel.reshape(-1)), axis=1).reshape(h, len(pairs), 2 * n_dc)
    bank = jnp.einsum("hnx,xqk->hnqk", lhs, jnp.asarray(rhs.reshape(2 * n_dc, GRID_W, 2 * GRID_W)),
                      precision=lax.Precision.HIGHEST)
    valid = (sel != n_dr)[:, None, :, None] & col_valid[None, :, None, :]
    return jnp.where(jnp.asarray(valid.reshape(len(pairs), GRID_W, 2 * GRID_W))[None], bank, NEG)


def _attn_a_kernel(sink_ref, q_ref, k_ref, v_ref, tbl_ref, o_ref, *, seq):
    blk = A_HALF_WINDOW
    nk = 3 * blk
    nblk = seq // blk
    g = pl.program_id(0)
    sinks = [sink_ref[g * A_GROUP + j] * LOG2E for j in range(A_GROUP)]

    def block(bb, i):
        qrow = pl.multiple_of(i * blk, blk)
        krow = pl.multiple_of(jnp.clip((i - 1) * blk, 0, seq - nk), blk)
        var = _edge_variant(i, nblk)
        q4 = q_ref[bb, pl.ds(qrow, blk), :]
        kb = k_ref[bb, pl.ds(krow, nk), :]
        v1 = _with_ones(v_ref[bb, pl.ds(krow, nk), :])
        outs = []
        for j in range(A_GROUP):
            s = lax.dot_general(q4[:, j * HEAD_DIM:(j + 1) * HEAD_DIM], kb, _NT, preferred_element_type=jnp.float32)
            t = s + tbl_ref[j, var]
            m = jnp.maximum(jnp.max(t, axis=-1, keepdims=True), sinks[j])
            e = jnp.exp2(t - m).astype(jnp.bfloat16)
            ol = jnp.dot(e, v1, preferred_element_type=jnp.float32)
            outs.append(ol[:, :HEAD_DIM] / (ol[:, HEAD_DIM:] + jnp.exp2(sinks[j] - m)))
        o_ref[bb, pl.ds(qrow, blk), :] = jnp.concatenate(outs, axis=1).astype(o_ref.dtype)

    iters = nblk // A_UNROLL

    def body(n, carry):
        bb, it = n // iters, n % iters
        for u in range(A_UNROLL):
            block(bb, it * A_UNROLL + u)
        return carry

    lax.fori_loop(0, q_ref.shape[0] * iters, body, 0)


def _attn_a(p3, sink, tbl):
    b, s, _ = p3.shape
    gw = A_GROUP * HEAD_DIM
    nb = math.gcd(b, ATTN_BATCH)
    return pl.pallas_call(
        functools.partial(_attn_a_kernel, seq=s),
        grid=(A_KV_HEADS, b // nb),
        in_specs=[
            pl.BlockSpec(memory_space=pltpu.SMEM),
            pl.BlockSpec((nb, s, gw), lambda g, bi: (bi, 0, g)),
            pl.BlockSpec((nb, s, HEAD_DIM), lambda g, bi: (bi, 0, A_K_COL + g)),
            pl.BlockSpec((nb, s, HEAD_DIM), lambda g, bi: (bi, 0, A_V_COL + g)),
            pl.BlockSpec((A_GROUP,) + tbl.shape[1:], lambda g, bi: (g, 0, 0, 0)),
        ],
        out_specs=pl.BlockSpec((nb, s, gw), lambda g, bi: (bi, 0, g)),
        out_shape=jax.ShapeDtypeStruct((b, s, A_Q_HEADS * HEAD_DIM), jnp.bfloat16),
        compiler_params=_cparams(("parallel", "parallel")),
        name="attn_a",
    )(sink, p3, p3, p3, tbl)


def _attn_b_kernel(*refs, seq):
    batch_rows = refs[0].shape[0]
    lax.fori_loop(0, batch_rows, functools.partial(_attn_b_one, refs=refs, seq=seq), 0)


def _attn_b_one(bb, carry, *, refs, seq):
    (q_ref, k_ref, v_ref, tbl_ref, o_ref, qf, kf, vf, q4, k4, v4, of,
     acc0, acc1, acc2, m0, m1, m2, l0, l1, l2) = refs
    qb_rows = 2 * B_HALF
    kw = 4 * B_HALF
    fold = B_CONFIGS[1][1]

    def attend(qb, kb, vb, tb):
        s = lax.dot_general(qb, kb, _NT, preferred_element_type=jnp.float32)
        t = s + tb
        m = jnp.max(t, axis=-1, keepdims=True)
        e = jnp.exp2(t - m).astype(jnp.bfloat16)
        al = jnp.dot(e, _with_ones(vb), preferred_element_type=jnp.float32)
        return al[:, :HEAD_DIM], jnp.broadcast_to(m, (qb.shape[0], HEAD_DIM)), al[:, HEAD_DIM:]

    nblk = seq // qb_rows

    def nat_block(i):
        qrow = pl.multiple_of(i * qb_rows, qb_rows)
        krow = pl.multiple_of(jnp.clip(i * qb_rows - B_HALF, 0, seq - kw), B_HALF)
        a, m, l = attend(q_ref[bb, pl.ds(qrow, qb_rows), :], k_ref[bb, pl.ds(krow, kw), :],
                         v_ref[bb, pl.ds(krow, kw), :], tbl_ref[0, _edge_variant(i, nblk)])
        acc0[pl.ds(qrow, qb_rows), :] = a
        m0[pl.ds(qrow, qb_rows), :] = m
        l0[pl.ds(qrow, qb_rows), :] = l

    def nat_body(it, carry):
        for u in range(B_UNROLL):
            nat_block(it * B_UNROLL + u)
        return carry

    lax.fori_loop(0, nblk // B_UNROLL, nat_body, 0)

    qf[...] = q_ref[bb].astype(jnp.float32)
    kf[...] = k_ref[bb].astype(jnp.float32)
    vf[...] = v_ref[bb].astype(jnp.float32)

    def dilated(d, var0, acc, mm, ll):
        ln = seq // d
        nb = ln // qb_rows
        per_iter = max(B_UNROLL // nb, 1)

        def residue(r):
            if d == fold:
                cls = [src[pl.ds(r, ln, stride=d), :] for src in (qf, kf, vf)]
                for dst, val in zip((q4, k4, v4), cls):
                    dst[pl.ds(pl.multiple_of(r * ln, ln), ln), :] = val
            else:
                first = (r % fold) * (seq // fold) + r // fold
                cls = [src[pl.ds(first, ln, stride=d // fold), :] for src in (q4, k4, v4)]
            qr, kr, vr = (val.astype(jnp.bfloat16) for val in cls)
            for i in range(nb):
                if nb == 1:
                    k0, width, var = 0, ln, var0
                else:
                    k0 = min(max(i * qb_rows - B_HALF, 0), ln - kw)
                    width = kw
                    var = var0 + (0 if i == 0 else (2 if i == nb - 1 else 1))
                a, m, l = attend(qr[i * qb_rows:(i + 1) * qb_rows], kr[k0:k0 + width], vr[k0:k0 + width],
                                 tbl_ref[0, var, :, :width])
                start = (r % fold) * (seq // fold) + r // fold + (d // fold) * qb_rows * i
                if d == fold:
                    rows = pl.ds(pl.multiple_of(start, qb_rows), qb_rows)
                else:
                    rows = pl.ds(start, qb_rows, stride=d // fold)
                acc[rows, :] = a
                mm[rows, :] = m
                ll[rows, :] = l

        def body(it, carry):
            for u in range(per_iter):
                residue(it * per_iter + u)
            return carry

        lax.fori_loop(0, d // per_iter, body, 0)

    dilated(B_CONFIGS[1][1], 3, acc1, m1, l1)
    dilated(B_CONFIGS[2][1], 6, acc2, m2, l2)

    merge_rows = 2 * qb_rows
    chunks_per_class = seq // fold // merge_rows

    def merge(c, carry):
        rows = pl.ds(pl.multiple_of(c * merge_rows, merge_rows), merge_rows)
        nat = pl.ds(c // chunks_per_class + fold * merge_rows * (c % chunks_per_class), merge_rows, stride=fold)
        ma, mb, mc = m0[nat, :], m1[rows, :], m2[rows, :]
        mx = jnp.maximum(jnp.maximum(ma, mb), mc)
        wa, wb, wc = jnp.exp2(ma - mx), jnp.exp2(mb - mx), jnp.exp2(mc - mx)
        num = wa * acc0[nat, :] + wb * acc1[rows, :] + wc * acc2[rows, :]
        den = wa * l0[nat, :] + wb * l1[rows, :] + wc * l2[rows, :]
        of[nat, :] = num / den
        return carry

    lax.fori_loop(0, seq // merge_rows, merge, 0)
    o_ref[bb] = of[...].astype(o_ref.dtype)
    return carry


def _attn_b(p3, tbl):
    b, s, _ = p3.shape
    f32_slab = pltpu.VMEM((s, HEAD_DIM), jnp.float32)
    nb = math.gcd(b, ATTN_BATCH)
    return pl.pallas_call(
        functools.partial(_attn_b_kernel, seq=s),
        grid=(B_HEADS, b // nb),
        in_specs=[
            pl.BlockSpec((nb, s, HEAD_DIM), lambda h, bi: (bi, 0, B_Q_COL + h)),
            pl.BlockSpec((nb, s, HEAD_DIM), lambda h, bi: (bi, 0, B_K_COL + h)),
            pl.BlockSpec((nb, s, HEAD_DIM), lambda h, bi: (bi, 0, B_V_COL + h)),
            pl.BlockSpec((1,) + tbl.shape[1:], lambda h, bi: (h, 0, 0, 0)),
        ],
        out_specs=pl.BlockSpec((nb, s, HEAD_DIM), lambda h, bi: (bi, 0, h)),
        out_shape=jax.ShapeDtypeStruct((b, s, B_HEADS * HEAD_DIM), jnp.bfloat16),
        scratch_shapes=[f32_slab] * 16,
        compiler_params=_cparams(("parallel", "parallel")),
        name="attn_b",
    )(p3, p3, p3, tbl)


def _attn_c_kernel(q_ref, k_ref, v_ref, bank_ref, o_ref, *, starts, slabs):
    nq = NA_QROWS * GRID_W
    nk = NA_KROWS * GRID_W
    pw = 2 * GRID_W

    def one(bb, carry):
        for g, (ws, per_row) in enumerate(zip(starts, slabs)):
            qb = q_ref[bb, g * nq:(g + 1) * nq, :]
            kb = k_ref[bb, ws * GRID_W:ws * GRID_W + nk, :]
            vb = v_ref[bb, ws * GRID_W:ws * GRID_W + nk, :]
            s = lax.dot_general(qb, kb, _NT, preferred_element_type=jnp.float32)
            t_rows = []
            for rr, row_pairs in enumerate(per_row):
                pieces = []
                for p, slab in enumerate(row_pairs):
                    width = min(pw, nk - p * pw)
                    pieces.append(s[rr * GRID_W:(rr + 1) * GRID_W, p * pw:p * pw + width]
                                  + bank_ref[0, slab, :, :width])
                t_rows.append(jnp.concatenate(pieces, axis=1))
            t = jnp.concatenate(t_rows, axis=0)
            m = jnp.max(t, axis=-1, keepdims=True)
            e = jnp.exp2(t - m).astype(jnp.bfloat16)
            ol = jnp.dot(e, _with_ones(vb), preferred_element_type=jnp.float32)
            o_ref[bb, g * nq:(g + 1) * nq, :] = (ol[:, :HEAD_DIM] / ol[:, HEAD_DIM:]).astype(o_ref.dtype)
        return carry

    lax.fori_loop(0, q_ref.shape[0], one, 0)


def _attn_c(p3, tbl, starts, slabs):
    b, s, _ = p3.shape
    nb = math.gcd(b, ATTN_BATCH)
    return pl.pallas_call(
        functools.partial(_attn_c_kernel, starts=starts, slabs=slabs),
        grid=(C_HEADS, b // nb),
        in_specs=[
            pl.BlockSpec((nb, s, HEAD_DIM), lambda h, bi: (bi, 0, h)),
            pl.BlockSpec((nb, s, HEAD_DIM), lambda h, bi: (bi, 0, C_HEADS + h)),
            pl.BlockSpec((nb, s, HEAD_DIM), lambda h, bi: (bi, 0, 2 * C_HEADS + h)),
            pl.BlockSpec((1,) + tbl.shape[1:], lambda h, bi: (h, 0, 0, 0)),
        ],
        out_specs=pl.BlockSpec((nb, s, HEAD_DIM), lambda h, bi: (bi, 0, h)),
        out_shape=jax.ShapeDtypeStruct((b, s, C_HEADS * HEAD_DIM), jnp.bfloat16),
        compiler_params=_cparams(("parallel", "parallel")),
        name="attn_c",
    )(p3, p3, p3, tbl)


def _gate_out_kernel(*refs, n_y, final_norm):
    y_refs = refs[:n_y]
    z_ref, w_ref, x_ref = refs[n_y:n_y + 3]
    o_ref = refs[-1]
    y = jnp.concatenate([r[...] for r in y_refs], axis=1)
    z = z_ref[...]
    gated = y * (z / (1.0 + jnp.exp(-z)))
    out = x_ref[...] + jnp.dot(gated, w_ref[...], preferred_element_type=jnp.float32)
    if final_norm:
        g_ref = refs[-2]
        ms = jnp.mean(out * out, axis=-1, keepdims=True)
        out = out * lax.rsqrt(ms + RMS_EPS) * g_ref[...]
    o_ref[...] = out


def _gate_out(ys, z2d, w, x2d, g_final, *, tm, name):
    m, d = x2d.shape
    in_specs = [pl.BlockSpec((tm, y.shape[1]), lambda i: (i, 0)) for y in ys]
    in_specs += [pl.BlockSpec((tm, d), lambda i: (i, 0)), pl.BlockSpec(w.shape, lambda i: (0, 0)),
                 pl.BlockSpec((tm, d), lambda i: (i, 0))]
    args = list(ys) + [z2d, w, x2d]
    if g_final is not None:
        in_specs.append(pl.BlockSpec((1, d), lambda i: (0, 0)))
        args.append(g_final.reshape(1, d))
    return pl.pallas_call(
        functools.partial(_gate_out_kernel, n_y=len(ys), final_norm=g_final is not None),
        grid=(m // tm,),
        in_specs=in_specs,
        out_specs=pl.BlockSpec((tm, d), lambda i: (i, 0)),
        out_shape=jax.ShapeDtypeStruct((m, d), jnp.float32),
        compiler_params=_cparams(("parallel",)),
        name=name,
    )(*args)


def kernel(x, ln_ab, w_in_ab, sink_a, w_out_ab, ln_c, w_in_c, rpb_c, w_out_c, ln_f):
    b, s, d = x.shape
    assert s == 2048 and d == 2048, "tiling below is written for SEQ = D_MODEL = 2048"
    assert ln_ab.shape[0] == 1 and ln_c.shape[0] == 1, "depth-2 trunk: one layer of each kind"
    m = b * s
    bf16 = jnp.bfloat16
    x2d = x.reshape(m, d)

    slopes = _alibi_slopes(A_Q_HEADS + B_HEADS)
    aw = A_HALF_WINDOW
    tbl_a = jnp.asarray(np.stack([_band_bias(aw, 3 * aw, off, aw, 1, slopes[:A_Q_HEADS]) for off in (0, aw, 2 * aw)],
                                 axis=1))
    qb, kw = 2 * B_HALF, 4 * B_HALF
    tbl_b = jnp.asarray(np.stack(
        [_band_bias(qb, kw, off, B_HALF, dil, slopes[A_Q_HEADS:]) for dil in (1, 4) for off in (0, B_HALF, 2 * B_HALF)]
        + [_band_bias(qb, kw, 0, B_HALF, B_CONFIGS[2][1], slopes[A_Q_HEADS:])], axis=1))
    na_starts, na_slabs, na_pairs = _na_plan(s // GRID_W)
    tbl_c = _na_bias_bank(rpb_c[0], na_pairs)

    hd = HEAD_DIM
    q_cols_ab = ((A_Q_COL * hd, A_K_COL * hd), (B_Q_COL * hd, B_K_COL * hd))
    p, z_ab = _norm_proj(x2d, ln_ab[0], w_in_ab[0].astype(bf16), q_cols_ab, tm=512, name="norm_proj_ab")
    p3 = p.reshape(b, s, p.shape[1])
    ya = _attn_a(p3, sink_a[0], tbl_a)
    yb = _attn_b(p3, tbl_b)
    x1 = _gate_out([ya.reshape(m, -1), yb.reshape(m, -1)], z_ab, w_out_ab[0].astype(bf16), x2d, None,
                   tm=512, name="gate_out_ab")

    pc, z_c = _norm_proj(x1, ln_c[0], w_in_c[0].astype(bf16), ((0, C_HEADS * hd),), tm=512, name="norm_proj_c")
    yc = _attn_c(pc.reshape(b, s, pc.shape[1]), tbl_c, na_starts, na_slabs)
    out = _gate_out([yc.reshape(m, -1)], z_c, w_out_c[0].astype(bf16), x1, ln_f, tm=512, name="gate_out_c")
    return out.reshape(b, s, d)
```

```python
import functools
import math

import numpy as np
import jax
import jax.numpy as jnp
from jax import lax
from jax.experimental import pallas as pl
from jax.experimental.pallas import tpu as pltpu

HEAD_DIM = 128
A_Q_HEADS = 8
A_KV_HEADS = 2
A_GROUP = A_Q_HEADS // A_KV_HEADS
A_HALF_WINDOW = 128
B_HEADS = 8
B_CONFIGS = ((128, 1), (512, 4), (2048, 16))
B_HALF = 64
C_HEADS = 16
GRID_W = 64
NA_KH = 8
NA_KW = 16
RMS_EPS = 1e-5
NEG = float("-inf")
LOG2E = 1.4426950408889634
QK_SCALE = HEAD_DIM ** -0.5 * LOG2E

A_Q_COL = 0
A_K_COL = A_Q_HEADS
A_V_COL = A_K_COL + A_KV_HEADS
B_Q_COL = A_V_COL + A_KV_HEADS
B_K_COL = B_Q_COL + B_HEADS
B_V_COL = B_K_COL + B_HEADS

NA_QROWS = 2
NA_KROWS = NA_QROWS + NA_KH - 1

A_UNROLL = 16
B_UNROLL = 16
ATTN_BATCH = 4

VMEM_LIMIT = 60 * 1024 * 1024

_NT = (((1,), (1,)), ((), ()))


def _cparams(sem):
    return pltpu.CompilerParams(dimension_semantics=sem, vmem_limit_bytes=VMEM_LIMIT)


def _edge_variant(i, n):
    return jnp.where(i == 0, 0, jnp.where(i == n - 1, 2, 1))


def _with_ones(vb):
    return jnp.concatenate([vb, jnp.ones_like(vb)], axis=1)


def _norm_proj_kernel(*refs, n_cast):
    x_ref, g_ref, w_ref, cs_ref = refs[:4]
    cast_in = refs[4:4 + n_cast]
    qkv_ref, z_ref = refs[4 + n_cast:6 + n_cast]
    cast_out = refs[6 + n_cast:]
    x = x_ref[...]
    ms = jnp.mean(x * x, axis=-1, keepdims=True)
    hn = (x * lax.rsqrt(ms + RMS_EPS) * g_ref[...]).astype(jnp.bfloat16)
    acc = jnp.dot(hn, w_ref[...], preferred_element_type=jnp.float32)
    res = (acc * cs_ref[...]).astype(qkv_ref.dtype)
    n_qkv = qkv_ref.shape[1]
    qkv_ref[...] = res[:, :n_qkv]
    z_ref[...] = res[:, n_qkv:]
    for src, dst in zip(cast_in, cast_out):
        dst[...] = src[...].astype(dst.dtype)


def _norm_proj(x2d, g, w, q_cols, *, tm, name, cast_along=()):
    m, d = x2d.shape
    n = w.shape[1]
    steps = m // tm
    cs = np.ones((1, n), np.float32)
    for lo, hi in q_cols:
        cs[:, lo:hi] = QK_SCALE
    cast_specs = [pl.BlockSpec((c.shape[0] // steps, c.shape[1]), lambda i: (i, 0)) for c in cast_along]
    return pl.pallas_call(
        functools.partial(_norm_proj_kernel, n_cast=len(cast_along)),
        grid=(steps,),
        in_specs=[
            pl.BlockSpec((tm, d), lambda i: (i, 0)),
            pl.BlockSpec((1, d), lambda i: (0, 0)),
            pl.BlockSpec((d, n), lambda i: (0, 0), pipeline_mode=pl.Buffered(1)),
            pl.BlockSpec((1, n), lambda i: (0, 0)),
        ] + cast_specs,
        out_specs=[pl.BlockSpec((tm, n - d), lambda i: (i, 0)), pl.BlockSpec((tm, d), lambda i: (i, 0))] + cast_specs,
        out_shape=[jax.ShapeDtypeStruct((m, n - d), jnp.bfloat16), jax.ShapeDtypeStruct((m, d), jnp.bfloat16)]
        + [jax.ShapeDtypeStruct(c.shape, jnp.bfloat16) for c in cast_along],
        compiler_params=_cparams(("parallel",)),
        name=name,
    )(x2d, g.reshape(1, d), w, jnp.asarray(cs), *cast_along)


def _alibi_slopes(n):
    return np.exp2(np.float32(-8.0) * np.arange(1, n + 1, dtype=np.float32) / np.float32(n))


def _band_bias(nq, nk, off, half, dist_scale, slopes):
    rel = np.arange(nk)[None, :] - np.arange(nq)[:, None] - off
    valid = np.abs(rel) <= half
    dist = (np.abs(rel) * dist_scale).astype(np.float32)
    bias = -slopes[:, None, None] * dist
    return np.where(valid[None], bias * np.float32(LOG2E), np.float32(NEG))


def _na_plan(rows):
    masked = 2 * NA_KH - 1
    starts, slabs, pairs = [], [], []
    for g in range(rows // NA_QROWS):
        ws = int(np.clip(g * NA_QROWS - NA_KH // 2, 0, rows - NA_KROWS))
        per_row = []
        for rr in range(NA_QROWS):
            r = g * NA_QROWS + rr
            r0 = int(np.clip(r - NA_KH // 2, 0, rows - NA_KH))
            assert ws <= r0 and r0 + NA_KH <= ws + NA_KROWS
            ids = [kr - r + NA_KH - 1 if r0 <= kr < r0 + NA_KH else masked for kr in range(ws, ws + NA_KROWS)]
            ids += [masked] * (len(ids) % 2)
            row_pairs = []
            for p in range(0, len(ids), 2):
                pair = (ids[p], ids[p + 1])
                if pair not in pairs:
                    pairs.append(pair)
                row_pairs.append(pairs.index(pair))
            per_row.append(tuple(row_pairs))
        starts.append(ws)
        slabs.append(tuple(per_row))
    return tuple(starts), tuple(slabs), tuple(pairs)


def _na_bias_bank(rpb, pairs):
    h, n_dr, n_dc = rpb.shape
    cq = np.arange(GRID_W)
    c0 = np.clip(cq - NA_KW // 2, 0, GRID_W - NA_KW)
    ck = np.arange(GRID_W)
    col_valid = (ck[None, :] >= c0[:, None]) & (ck[None, :] < c0[:, None] + NA_KW)
    dc = ck[None, :] - cq[:, None] + NA_KW - 1
    onehot = (dc[None] == np.arange(n_dc)[:, None, None]) & col_valid[None]
    rhs = np.zeros((2, n_dc, GRID_W, 2, GRID_W), np.float32)
    for j in range(2):
        rhs[j, :, :, j, :] = onehot
    sel = np.asarray(pairs, np.int32)
    rp = jnp.concatenate([rpb * LOG2E, jnp.zeros((h, 1, n_dc), jnp.float32)], axis=1)
    lhs = jnp.take(rp, jnp.asarray(sel.reshape(-1)), axis=1).reshape(h, len(pairs), 2 * n_dc)
    bank = jnp.einsum("hnx,xqk->hnqk", lhs, jnp.asarray(rhs.reshape(2 * n_dc, GRID_W, 2 * GRID_W)),
                      precision=lax.Precision.HIGHEST)
    valid = (sel != n_dr)[:, None, :, None] & col_valid[None, :, None, :]
    return jnp.where(jnp.asarray(valid.reshape(len(pairs), GRID_W, 2 * GRID_W))[None], bank, NEG)


def _attn_a_kernel(sink_ref, q_ref, k_ref, v_ref, tbl_ref, o_ref, *, seq):
    blk = A_HALF_WINDOW
    nk = 3 * blk
    nblk = seq // blk
    g = pl.program_id(0)
    sinks = [sink_ref[g * A_GROUP + j] * LOG2E for j in range(A_GROUP)]

    def block(bb, i):
        qrow = pl.multiple_of(i * blk, blk)
        krow = pl.multiple_of(jnp.clip((i - 1) * blk, 0, seq - nk), blk)
        var = _edge_variant(i, nblk)
        q4 = q_ref[bb, pl.ds(qrow, blk), :]
        kb = k_ref[bb, pl.ds(krow, nk), :]
        v1 = _with_ones(v_ref[bb, pl.ds(krow, nk), :])
        outs = []
        for j in range(A_GROUP):
            s = lax.dot_general(q4[:, j * HEAD_DIM:(j + 1) * HEAD_DIM], kb, _NT, preferred_element_type=jnp.float32)
            t = s + tbl_ref[j, var]
            m = jnp.maximum(jnp.max(t, axis=-1, keepdims=True), sinks[j])
            e = jnp.exp2(t - m).astype(jnp.bfloat16)
            ol = jnp.dot(e, v1, preferred_element_type=jnp.float32)
            outs.append(ol[:, :HEAD_DIM] / (ol[:, HEAD_DIM:] + jnp.exp2(sinks[j] - m)))
        o_ref[bb, pl.ds(qrow, blk), :] = jnp.concatenate(outs, axis=1).astype(o_ref.dtype)

    iters = nblk // A_UNROLL

    def body(n, carry):
        bb, it = n // iters, n % iters
        for u in range(A_UNROLL):
            block(bb, it * A_UNROLL + u)
        return carry

    lax.fori_loop(0, q_ref.shape[0] * iters, body, 0)


def _attn_a(p3, sink, tbl):
    b, s, _ = p3.shape
    gw = A_GROUP * HEAD_DIM
    nb = math.gcd(b, ATTN_BATCH)
    return pl.pallas_call(
        functools.partial(_attn_a_kernel, seq=s),
        grid=(A_KV_HEADS, b // nb),
        in_specs=[
            pl.BlockSpec(memory_space=pltpu.SMEM),
            pl.BlockSpec((nb, s, gw), lambda g, bi: (bi, 0, g)),
            pl.BlockSpec((nb, s, HEAD_DIM), lambda g, bi: (bi, 0, A_K_COL + g)),
            pl.BlockSpec((nb, s, HEAD_DIM), lambda g, bi: (bi, 0, A_V_COL + g)),
            pl.BlockSpec((A_GROUP,) + tbl.shape[1:], lambda g, bi: (g, 0, 0, 0)),
        ],
        out_specs=pl.BlockSpec((nb, s, gw), lambda g, bi: (bi, 0, g)),
        out_shape=jax.ShapeDtypeStruct((b, s, A_Q_HEADS * HEAD_DIM), jnp.bfloat16),
        compiler_params=_cparams(("parallel", "parallel")),
        name="attn_a",
    )(sink, p3, p3, p3, tbl)


def _attn_b_kernel(*refs, seq):
    batch_rows = refs[0].shape[0]
    lax.fori_loop(0, batch_rows, functools.partial(_attn_b_one, refs=refs, seq=seq), 0)


def _attn_b_one(bb, carry, *, refs, seq):
    (q_ref, k_ref, v_ref, tbl_ref, o_ref, qf, kf, vf, q4, k4, v4, of,
     acc0, acc1, acc2, m0, m1, m2, l0, l1, l2) = refs
    qb_rows = 2 * B_HALF
    kw = 4 * B_HALF
    fold = B_CONFIGS[1][1]

    def attend(qb, kb, vb, tb):
        s = lax.dot_general(qb, kb, _NT, preferred_element_type=jnp.float32)
        t = s + tb
        m = jnp.max(t, axis=-1, keepdims=True)
        e = jnp.exp2(t - m).astype(jnp.bfloat16)
        al = jnp.dot(e, _with_ones(vb), preferred_element_type=jnp.float32)
        return al[:, :HEAD_DIM], jnp.broadcast_to(m, (qb.shape[0], HEAD_DIM)), al[:, HEAD_DIM:]

    nblk = seq // qb_rows

    def nat_block(i):
        qrow = pl.multiple_of(i * qb_rows, qb_rows)
        krow = pl.multiple_of(jnp.clip(i * qb_rows - B_HALF, 0, seq - kw), B_HALF)
        a, m, l = attend(q_ref[bb, pl.ds(qrow, qb_rows), :], k_ref[bb, pl.ds(krow, kw), :],
                         v_ref[bb, pl.ds(krow, kw), :], tbl_ref[0, _edge_variant(i, nblk)])
        acc0[pl.ds(qrow, qb_rows), :] = a
        m0[pl.ds(qrow, qb_rows), :] = m
        l0[pl.ds(qrow, qb_rows), :] = l

    def nat_body(it, carry):
        for u in range(B_UNROLL):
            nat_block(it * B_UNROLL + u)
        return carry

    lax.fori_loop(0, nblk // B_UNROLL, nat_body, 0)

    qf[...] = q_ref[bb].astype(jnp.float32)
    kf[...] = k_ref[bb].astype(jnp.float32)
    vf[...] = v_ref[bb].astype(jnp.float32)

    def dilated(d, var0, acc, mm, ll):
        ln = seq // d
        nb = ln // qb_rows
        per_iter = max(B_UNROLL // nb, 1)

        def residue(r):
            if d == fold:
                cls = [src[pl.ds(r, ln, stride=d), :] for src in (qf, kf, vf)]
                for dst, val in zip((q4, k4, v4), cls):
                    dst[pl.ds(pl.multiple_of(r * ln, ln), ln), :] = val
            else:
                first = (r % fold) * (seq // fold) + r // fold
                cls = [src[pl.ds(first, ln, stride=d // fold), :] for src in (q4, k4, v4)]
            qr, kr, vr = (val.astype(jnp.bfloat16) for val in cls)
            for i in range(nb):
                if nb == 1:
                    k0, width, var = 0, ln, var0
                else:
                    k0 = min(max(i * qb_rows - B_HALF, 0), ln - kw)
                    width = kw
                    var = var0 + (0 if i == 0 else (2 if i == nb - 1 else 1))
                a, m, l = attend(qr[i * qb_rows:(i + 1) * qb_rows], kr[k0:k0 + width], vr[k0:k0 + width],
                                 tbl_ref[0, var, :, :width])
                start = (r % fold) * (seq // fold) + r // fold + (d // fold) * qb_rows * i
                if d == fold:
                    rows = pl.ds(pl.multiple_of(start, qb_rows), qb_rows)
                else:
                    rows = pl.ds(start, qb_rows, stride=d // fold)
                acc[rows, :] = a
                mm[rows, :] = m
                ll[rows, :] = l

        def body(it, carry):
            for u in range(per_iter):
                residue(it * per_iter + u)
            return carry

        lax.fori_loop(0, d // per_iter, body, 0)

    dilated(B_CONFIGS[1][1], 3, acc1, m1, l1)
    dilated(B_CONFIGS[2][1], 6, acc2, m2, l2)

    merge_rows = 2 * qb_rows
    chunks_per_class = seq // fold // merge_rows

    def merge(c, carry):
        rows = pl.ds(pl.multiple_of(c * merge_rows, merge_rows), merge_rows)
        nat = pl.ds(c // chunks_per_class + fold * merge_rows * (c % chunks_per_class), merge_rows, stride=fold)
        ma, mb, mc = m0[nat, :], m1[rows, :], m2[rows, :]
        mx = jnp.maximum(jnp.maximum(ma, mb), mc)
        wa, wb, wc = jnp.exp2(ma - mx), jnp.exp2(mb - mx), jnp.exp2(mc - mx)
        num = wa * acc0[nat, :] + wb * acc1[rows, :] + wc * acc2[rows, :]
        den = wa * l0[nat, :] + wb * l1[rows, :] + wc * l2[rows, :]
        of[nat, :] = num / den
        return carry

    lax.fori_loop(0, seq // merge_rows, merge, 0)
    o_ref[bb] = of[...].astype(o_ref.dtype)
    return carry


def _attn_b(p3, tbl):
    b, s, _ = p3.shape
    f32_slab = pltpu.VMEM((s, HEAD_DIM), jnp.float32)
    nb = math.gcd(b, ATTN_BATCH)
    return pl.pallas_call(
        functools.partial(_attn_b_kernel, seq=s),
        grid=(B_HEADS, b // nb),
        in_specs=[
            pl.BlockSpec((nb, s, HEAD_DIM), lambda h, bi: (bi, 0, B_Q_COL + h)),
            pl.BlockSpec((nb, s, HEAD_DIM), lambda h, bi: (bi, 0, B_K_COL + h)),
            pl.BlockSpec((nb, s, HEAD_DIM), lambda h, bi: (bi, 0, B_V_COL + h)),
            pl.BlockSpec((1,) + tbl.shape[1:], lambda h, bi: (h, 0, 0, 0)),
        ],
        out_specs=pl.BlockSpec((nb, s, HEAD_DIM), lambda h, bi: (bi, 0, h)),
        out_shape=jax.ShapeDtypeStruct((b, s, B_HEADS * HEAD_DIM), jnp.bfloat16),
        scratch_shapes=[f32_slab] * 16,
        compiler_params=_cparams(("parallel", "parallel")),
        name="attn_b",
    )(p3, p3, p3, tbl)


def _attn_c_kernel(q_ref, k_ref, v_ref, bank_ref, o_ref, *, starts, slabs):
    nq = NA_QROWS * GRID_W
    nk = NA_KROWS * GRID_W
    pw = 2 * GRID_W

    def one(bb, carry):
        for g, (ws, per_row) in enumerate(zip(starts, slabs)):
            qb = q_ref[bb, g * nq:(g + 1) * nq, :]
            kb = k_ref[bb, ws * GRID_W:ws * GRID_W + nk, :]
            vb = v_ref[bb, ws * GRID_W:ws * GRID_W + nk, :]
            s = lax.dot_general(qb, kb, _NT, preferred_element_type=jnp.float32)
            t_rows = []
            for rr, row_pairs in enumerate(per_row):
                pieces = []
                for p, slab in enumerate(row_pairs):
                    width = min(pw, nk - p * pw)
                    pieces.append(s[rr * GRID_W:(rr + 1) * GRID_W, p * pw:p * pw + width]
                                  + bank_ref[0, slab, :, :width])
                t_rows.append(jnp.concatenate(pieces, axis=1))
            t = jnp.concatenate(t_rows, axis=0)
            m = jnp.max(t, axis=-1, keepdims=True)
            e = jnp.exp2(t - m).astype(jnp.bfloat16)
            ol = jnp.dot(e, _with_ones(vb), preferred_element_type=jnp.float32)
            o_ref[bb, g * nq:(g + 1) * nq, :] = (ol[:, :HEAD_DIM] / ol[:, HEAD_DIM:]).astype(o_ref.dtype)
        return carry

    lax.fori_loop(0, q_ref.shape[0], one, 0)


def _attn_c(p3, tbl, starts, slabs):
    b, s, _ = p3.shape
    nb = math.gcd(b, ATTN_BATCH)
    return pl.pallas_call(
        functools.partial(_attn_c_kernel, starts=starts, slabs=slabs),
        grid=(C_HEADS, b // nb),
        in_specs=[
            pl.BlockSpec((nb, s, HEAD_DIM), lambda h, bi: (bi, 0, h)),
            pl.BlockSpec((nb, s, HEAD_DIM), lambda h, bi: (bi, 0, C_HEADS + h)),
            pl.BlockSpec((nb, s, HEAD_DIM), lambda h, bi: (bi, 0, 2 * C_HEADS + h)),
            pl.BlockSpec((1,) + tbl.shape[1:], lambda h, bi: (h, 0, 0, 0)),
        ],
        out_specs=pl.BlockSpec((nb, s, HEAD_DIM), lambda h, bi: (bi, 0, h)),
        out_shape=jax.ShapeDtypeStruct((b, s, C_HEADS * HEAD_DIM), jnp.bfloat16),
        compiler_params=_cparams(("parallel", "parallel")),
        name="attn_c",
    )(p3, p3, p3, tbl)


def _gate_out_kernel(*refs, n_y, final_norm):
    y_refs = refs[:n_y]
    z_ref, w_ref, x_ref = refs[n_y:n_y + 3]
    o_ref = refs[-1]
    y = jnp.concatenate([r[...] for r in y_refs], axis=1)
    z = z_ref[...]
    gated = y * (z / (1.0 + jnp.exp(-z)))
    out = x_ref[...] + jnp.dot(gated, w_ref[...], preferred_element_type=jnp.float32)
    if final_norm:
        g_ref = refs[-2]
        ms = jnp.mean(out * out, axis=-1, keepdims=True)
        out = out * lax.rsqrt(ms + RMS_EPS) * g_ref[...]
    o_ref[...] = out


def _gate_out(ys, z2d, w, x2d, g_final, *, tm, name):
    m, d = x2d.shape
    in_specs = [pl.BlockSpec((tm, y.shape[1]), lambda i: (i, 0)) for y in ys]
    in_specs += [pl.BlockSpec((tm, d), lambda i: (i, 0)), pl.BlockSpec(w.shape, lambda i: (0, 0)),
                 pl.BlockSpec((tm, d), lambda i: (i, 0))]
    args = list(ys) + [z2d, w, x2d]
    if g_final is not None:
        in_specs.append(pl.BlockSpec((1, d), lambda i: (0, 0)))
        args.append(g_final.reshape(1, d))
    return pl.pallas_call(
        functools.partial(_gate_out_kernel, n_y=len(ys), final_norm=g_final is not None),
        grid=(m // tm,),
        in_specs=in_specs,
        out_specs=pl.BlockSpec((tm, d), lambda i: (i, 0)),
        out_shape=jax.ShapeDtypeStruct((m, d), jnp.float32),
        compiler_params=_cparams(("parallel",)),
        name=name,
    )(*args)


def kernel(x, ln_ab, w_in_ab, sink_a, w_out_ab, ln_c, w_in_c, rpb_c, w_out_c, ln_f):
    b, s, d = x.shape
    assert s == 2048 and d == 2048, "tiling below is written for SEQ = D_MODEL = 2048"
    assert ln_ab.shape[0] == 1 and ln_c.shape[0] == 1, "depth-2 trunk: one layer of each kind"
    m = b * s
    bf16 = jnp.bfloat16
    x2d = x.reshape(m, d)

    slopes = _alibi_slopes(A_Q_HEADS + B_HEADS)
    aw = A_HALF_WINDOW
    tbl_a = jnp.asarray(np.stack([_band_bias(aw, 3 * aw, off, aw, 1, slopes[:A_Q_HEADS]) for off in (0, aw, 2 * aw)],
                                 axis=1))
    qb, kw = 2 * B_HALF, 4 * B_HALF
    tbl_b = jnp.asarray(np.stack(
        [_band_bias(qb, kw, off, B_HALF, dil, slopes[A_Q_HEADS:]) for dil in (1, 4) for off in (0, B_HALF, 2 * B_HALF)]
        + [_band_bias(qb, kw, 0, B_HALF, B_CONFIGS[2][1], slopes[A_Q_HEADS:])], axis=1))
    na_starts, na_slabs, na_pairs = _na_plan(s // GRID_W)
    tbl_c = _na_bias_bank(rpb_c[0], na_pairs)

    hd = HEAD_DIM
    q_cols_ab = ((A_Q_COL * hd, A_K_COL * hd), (B_Q_COL * hd, B_K_COL * hd))
    p, z_ab, w_out_ab16, w_in_c16, w_out_c16 = _norm_proj(
        x2d, ln_ab[0], w_in_ab[0].astype(bf16), q_cols_ab, tm=512, name="norm_proj_ab",
        cast_along=(w_out_ab[0], w_in_c[0], w_out_c[0]))
    p3 = p.reshape(b, s, p.shape[1])
    ya = _attn_a(p3, sink_a[0], tbl_a)
    yb = _attn_b(p3, tbl_b)
    x1 = _gate_out([ya.reshape(m, -1), yb.reshape(m, -1)], z_ab, w_out_ab16, x2d, None, tm=512, name="gate_out_ab")

    pc, z_c = _norm_proj(x1, ln_c[0], w_in_c16, ((0, C_HEADS * hd),), tm=512, name="norm_proj_c")
    yc = _attn_c(pc.reshape(b, s, pc.shape[1]), tbl_c, na_starts, na_slabs)
    out = _gate_out([yc.reshape(m, -1)], z_c, w_out_c16, x1, ln_f, tm=512, name="gate_out_c")
    return out.reshape(b, s, d)
```

```python
import functools
import math

import numpy as np
import jax
import jax.numpy as jnp
from jax import lax
from jax.experimental import pallas as pl
from jax.experimental.pallas import tpu as pltpu

HEAD_DIM = 128
A_Q_HEADS = 8
A_KV_HEADS = 2
A_GROUP = A_Q_HEADS // A_KV_HEADS
A_HALF_WINDOW = 128
B_HEADS = 8
B_CONFIGS = ((128, 1), (512, 4), (2048, 16))
B_HALF = 64
C_HEADS = 16
GRID_W = 64
NA_KH = 8
NA_KW = 16
RMS_EPS = 1e-5
NEG = float("-inf")
LOG2E = 1.4426950408889634
QK_SCALE = HEAD_DIM ** -0.5 * LOG2E

A_Q_COL = 0
A_K_COL = A_Q_HEADS
A_V_COL = A_K_COL + A_KV_HEADS
B_Q_COL = A_V_COL + A_KV_HEADS
B_K_COL = B_Q_COL + B_HEADS
B_V_COL = B_K_COL + B_HEADS

NA_QROWS = 2
NA_KROWS = NA_QROWS + NA_KH - 1

ATTN_BATCH = 4

VMEM_LIMIT = 60 * 1024 * 1024

_NT = (((1,), (1,)), ((), ()))


def _cparams(sem):
    return pltpu.CompilerParams(dimension_semantics=sem, vmem_limit_bytes=VMEM_LIMIT)


def _edge_variant(i, n):
    return 0 if i == 0 else (2 if i == n - 1 else 1)


def _with_ones(vb):
    return jnp.concatenate([vb, jnp.ones_like(vb)], axis=1)


def _norm_proj_kernel(*refs, n_cast):
    x_ref, g_ref, w_ref, cs_ref = refs[:4]
    cast_in = refs[4:4 + n_cast]
    qkv_ref, z_ref = refs[4 + n_cast:6 + n_cast]
    cast_out = refs[6 + n_cast:]
    x = x_ref[...]
    ms = jnp.mean(x * x, axis=-1, keepdims=True)
    hn = (x * lax.rsqrt(ms + RMS_EPS) * g_ref[...]).astype(jnp.bfloat16)
    acc = jnp.dot(hn, w_ref[...], preferred_element_type=jnp.float32)
    res = (acc * cs_ref[...]).astype(qkv_ref.dtype)
    n_qkv = qkv_ref.shape[1]
    qkv_ref[...] = res[:, :n_qkv]
    z_ref[...] = res[:, n_qkv:]
    for src, dst in zip(cast_in, cast_out):
        dst[...] = src[...].astype(dst.dtype)


def _norm_proj(x2d, g, w, q_cols, *, tm, name, cast_along=()):
    m, d = x2d.shape
    n = w.shape[1]
    steps = m // tm
    cs = np.ones((1, n), np.float32)
    for lo, hi in q_cols:
        cs[:, lo:hi] = QK_SCALE
    cast_specs = [pl.BlockSpec((c.shape[0] // steps, c.shape[1]), lambda i: (i, 0)) for c in cast_along]
    return pl.pallas_call(
        functools.partial(_norm_proj_kernel, n_cast=len(cast_along)),
        grid=(steps,),
        in_specs=[
            pl.BlockSpec((tm, d), lambda i: (i, 0)),
            pl.BlockSpec((1, d), lambda i: (0, 0)),
            pl.BlockSpec((d, n), lambda i: (0, 0), pipeline_mode=pl.Buffered(1)),
            pl.BlockSpec((1, n), lambda i: (0, 0)),
        ] + cast_specs,
        out_specs=[pl.BlockSpec((tm, n - d), lambda i: (i, 0)), pl.BlockSpec((tm, d), lambda i: (i, 0))] + cast_specs,
        out_shape=[jax.ShapeDtypeStruct((m, n - d), jnp.bfloat16), jax.ShapeDtypeStruct((m, d), jnp.bfloat16)]
        + [jax.ShapeDtypeStruct(c.shape, jnp.bfloat16) for c in cast_along],
        compiler_params=_cparams(("parallel",)),
        name=name,
    )(x2d, g.reshape(1, d), w, jnp.asarray(cs), *cast_along)


def _alibi_slopes(n):
    return np.exp2(np.float32(-8.0) * np.arange(1, n + 1, dtype=np.float32) / np.float32(n))


def _band_bias(nq, nk, off, half, dist_scale, slopes):
    rel = np.arange(nk)[None, :] - np.arange(nq)[:, None] - off
    valid = np.abs(rel) <= half
    dist = (np.abs(rel) * dist_scale).astype(np.float32)
    bias = -slopes[:, None, None] * dist
    return np.where(valid[None], bias * np.float32(LOG2E), np.float32(NEG))


def _na_plan(rows):
    masked = 2 * NA_KH - 1
    starts, slabs, pairs = [], [], []
    for g in range(rows // NA_QROWS):
        ws = int(np.clip(g * NA_QROWS - NA_KH // 2, 0, rows - NA_KROWS))
        per_row = []
        for rr in range(NA_QROWS):
            r = g * NA_QROWS + rr
            r0 = int(np.clip(r - NA_KH // 2, 0, rows - NA_KH))
            assert ws <= r0 and r0 + NA_KH <= ws + NA_KROWS
            ids = [kr - r + NA_KH - 1 if r0 <= kr < r0 + NA_KH else masked for kr in range(ws, ws + NA_KROWS)]
            ids += [masked] * (len(ids) % 2)
            row_pairs = []
            for p in range(0, len(ids), 2):
                pair = (ids[p], ids[p + 1])
                if pair not in pairs:
                    pairs.append(pair)
                row_pairs.append(pairs.index(pair))
            per_row.append(tuple(row_pairs))
        starts.append(ws)
        slabs.append(tuple(per_row))
    return tuple(starts), tuple(slabs), tuple(pairs)


def _na_bias_bank(rpb, pairs):
    h, n_dr, n_dc = rpb.shape
    cq = np.arange(GRID_W)
    c0 = np.clip(cq - NA_KW // 2, 0, GRID_W - NA_KW)
    ck = np.arange(GRID_W)
    col_valid = (ck[None, :] >= c0[:, None]) & (ck[None, :] < c0[:, None] + NA_KW)
    dc = ck[None, :] - cq[:, None] + NA_KW - 1
    onehot = (dc[None] == np.arange(n_dc)[:, None, None]) & col_valid[None]
    rhs = np.zeros((2, n_dc, GRID_W, 2, GRID_W), np.float32)
    for j in range(2):
        rhs[j, :, :, j, :] = onehot
    sel = np.asarray(pairs, np.int32)
    rp = jnp.concatenate([rpb * LOG2E, jnp.zeros((h, 1, n_dc), jnp.float32)], axis=1)
    lhs = jnp.take(rp, jnp.asarray(sel.reshape(-1)), axis=1).reshape(h, len(pairs), 2 * n_dc)
    bank = jnp.einsum("hnx,xqk->hnqk", lhs, jnp.asarray(rhs.reshape(2 * n_dc, GRID_W, 2 * GRID_W)),
                      precision=lax.Precision.HIGHEST)
    valid = (sel != n_dr)[:, None, :, None] & col_valid[None, :, None, :]
    return jnp.where(jnp.asarray(valid.reshape(len(pairs), GRID_W, 2 * GRID_W))[None], bank, NEG)


def _attn_a_kernel(sink_ref, q_ref, k_ref, v_ref, tbl_ref, o_ref, *, seq):
    blk = A_HALF_WINDOW
    nk = 3 * blk
    nblk = seq // blk
    g = pl.program_id(0)
    sinks = [sink_ref[g * A_GROUP + j] * LOG2E for j in range(A_GROUP)]

    def block(bb, i):
        qrow = i * blk
        krow = min(max((i - 1) * blk, 0), seq - nk)
        var = _edge_variant(i, nblk)
        q4 = q_ref[bb, pl.ds(qrow, blk), :]
        kb = k_ref[bb, pl.ds(krow, nk), :]
        v1 = _with_ones(v_ref[bb, pl.ds(krow, nk), :])
        outs = []
        for j in range(A_GROUP):
            s = lax.dot_general(q4[:, j * HEAD_DIM:(j + 1) * HEAD_DIM], kb, _NT, preferred_element_type=jnp.float32)
            t = s + tbl_ref[j, var]
            m = jnp.maximum(jnp.max(t, axis=-1, keepdims=True), sinks[j])
            e = jnp.exp2(t - m).astype(jnp.bfloat16)
            ol = jnp.dot(e, v1, preferred_element_type=jnp.float32)
            outs.append(ol[:, :HEAD_DIM] / (ol[:, HEAD_DIM:] + jnp.exp2(sinks[j] - m)))
        o_ref[bb, pl.ds(qrow, blk), :] = jnp.concatenate(outs, axis=1).astype(o_ref.dtype)

    def body(bb, carry):
        for i in range(nblk):
            block(bb, i)
        return carry

    lax.fori_loop(0, q_ref.shape[0], body, 0)


def _attn_a(p3, sink, tbl):
    b, s, _ = p3.shape
    gw = A_GROUP * HEAD_DIM
    nb = math.gcd(b, ATTN_BATCH)
    return pl.pallas_call(
        functools.partial(_attn_a_kernel, seq=s),
        grid=(A_KV_HEADS, b // nb),
        in_specs=[
            pl.BlockSpec(memory_space=pltpu.SMEM),
            pl.BlockSpec((nb, s, gw), lambda g, bi: (bi, 0, g)),
            pl.BlockSpec((nb, s, HEAD_DIM), lambda g, bi: (bi, 0, A_K_COL + g)),
            pl.BlockSpec((nb, s, HEAD_DIM), lambda g, bi: (bi, 0, A_V_COL + g)),
            pl.BlockSpec((A_GROUP,) + tbl.shape[1:], lambda g, bi: (g, 0, 0, 0)),
        ],
        out_specs=pl.BlockSpec((nb, s, gw), lambda g, bi: (bi, 0, g)),
        out_shape=jax.ShapeDtypeStruct((b, s, A_Q_HEADS * HEAD_DIM), jnp.bfloat16),
        compiler_params=_cparams(("parallel", "parallel")),
        name="attn_a",
    )(sink, p3, p3, p3, tbl)


def _attn_b_kernel(*refs, seq):
    batch_rows = refs[0].shape[0]
    lax.fori_loop(0, batch_rows, functools.partial(_attn_b_one, refs=refs, seq=seq), 0)


def _attn_b_one(bb, carry, *, refs, seq):
    (q_ref, k_ref, v_ref, tbl_ref, o_ref, qf, kf, vf, q4, k4, v4, of,
     acc0, acc1, acc2, m0, m1, m2, l0, l1, l2) = refs
    qb_rows = 2 * B_HALF
    kw = 4 * B_HALF
    fold = B_CONFIGS[1][1]

    def attend(qb, kb, vb, tb):
        s = lax.dot_general(qb, kb, _NT, preferred_element_type=jnp.float32)
        t = s + tb
        m = jnp.max(t, axis=-1, keepdims=True)
        e = jnp.exp2(t - m).astype(jnp.bfloat16)
        al = jnp.dot(e, _with_ones(vb), preferred_element_type=jnp.float32)
        return al[:, :HEAD_DIM], jnp.broadcast_to(m, (qb.shape[0], HEAD_DIM)), al[:, HEAD_DIM:]

    nblk = seq // qb_rows

    for i in range(nblk):
        qrow = i * qb_rows
        krow = min(max(i * qb_rows - B_HALF, 0), seq - kw)
        a, m, l = attend(q_ref[bb, pl.ds(qrow, qb_rows), :], k_ref[bb, pl.ds(krow, kw), :],
                         v_ref[bb, pl.ds(krow, kw), :], tbl_ref[0, _edge_variant(i, nblk)])
        acc0[pl.ds(qrow, qb_rows), :] = a
        m0[pl.ds(qrow, qb_rows), :] = m
        l0[pl.ds(qrow, qb_rows), :] = l

    qf[...] = q_ref[bb].astype(jnp.float32)
    kf[...] = k_ref[bb].astype(jnp.float32)
    vf[...] = v_ref[bb].astype(jnp.float32)

    def dilated(d, var0, acc, mm, ll):
        ln = seq // d
        nb = ln // qb_rows

        for r in range(d):
            if d == fold:
                cls = [src[pl.ds(r, ln, stride=d), :] for src in (qf, kf, vf)]
                for dst, val in zip((q4, k4, v4), cls):
                    dst[pl.ds(r * ln, ln), :] = val
            else:
                first = (r % fold) * (seq // fold) + r // fold
                cls = [src[pl.ds(first, ln, stride=d // fold), :] for src in (q4, k4, v4)]
            qr, kr, vr = (val.astype(jnp.bfloat16) for val in cls)
            for i in range(nb):
                if nb == 1:
                    k0, width, var = 0, ln, var0
                else:
                    k0 = min(max(i * qb_rows - B_HALF, 0), ln - kw)
                    width = kw
                    var = var0 + (0 if i == 0 else (2 if i == nb - 1 else 1))
                a, m, l = attend(qr[i * qb_rows:(i + 1) * qb_rows], kr[k0:k0 + width], vr[k0:k0 + width],
                                 tbl_ref[0, var, :, :width])
                start = (r % fold) * (seq // fold) + r // fold + (d // fold) * qb_rows * i
                rows = pl.ds(start, qb_rows) if d == fold else pl.ds(start, qb_rows, stride=d // fold)
                acc[rows, :] = a
                mm[rows, :] = m
                ll[rows, :] = l

    dilated(B_CONFIGS[1][1], 3, acc1, m1, l1)
    dilated(B_CONFIGS[2][1], 6, acc2, m2, l2)

    merge_rows = 2 * qb_rows
    chunks_per_class = seq // fold // merge_rows

    def merge(c, carry):
        rows = pl.ds(pl.multiple_of(c * merge_rows, merge_rows), merge_rows)
        nat = pl.ds(c // chunks_per_class + fold * merge_rows * (c % chunks_per_class), merge_rows, stride=fold)
        ma, mb, mc = m0[nat, :], m1[rows, :], m2[rows, :]
        mx = jnp.maximum(jnp.maximum(ma, mb), mc)
        wa, wb, wc = jnp.exp2(ma - mx), jnp.exp2(mb - mx), jnp.exp2(mc - mx)
        num = wa * acc0[nat, :] + wb * acc1[rows, :] + wc * acc2[rows, :]
        den = wa * l0[nat, :] + wb * l1[rows, :] + wc * l2[rows, :]
        of[nat, :] = num / den
        return carry

    lax.fori_loop(0, seq // merge_rows, merge, 0)
    o_ref[bb] = of[...].astype(o_ref.dtype)
    return carry


def _attn_b(p3, tbl):
    b, s, _ = p3.shape
    f32_slab = pltpu.VMEM((s, HEAD_DIM), jnp.float32)
    nb = math.gcd(b, ATTN_BATCH)
    return pl.pallas_call(
        functools.partial(_attn_b_kernel, seq=s),
        grid=(B_HEADS, b // nb),
        in_specs=[
            pl.BlockSpec((nb, s, HEAD_DIM), lambda h, bi: (bi, 0, B_Q_COL + h)),
            pl.BlockSpec((nb, s, HEAD_DIM), lambda h, bi: (bi, 0, B_K_COL + h)),
            pl.BlockSpec((nb, s, HEAD_DIM), lambda h, bi: (bi, 0, B_V_COL + h)),
            pl.BlockSpec((1,) + tbl.shape[1:], lambda h, bi: (h, 0, 0, 0)),
        ],
        out_specs=pl.BlockSpec((nb, s, HEAD_DIM), lambda h, bi: (bi, 0, h)),
        out_shape=jax.ShapeDtypeStruct((b, s, B_HEADS * HEAD_DIM), jnp.bfloat16),
        scratch_shapes=[f32_slab] * 16,
        compiler_params=_cparams(("parallel", "parallel")),
        name="attn_b",
    )(p3, p3, p3, tbl)


def _attn_c_kernel(q_ref, k_ref, v_ref, bank_ref, o_ref, *, starts, slabs):
    nq = NA_QROWS * GRID_W
    nk = NA_KROWS * GRID_W
    pw = 2 * GRID_W

    def one(bb, carry):
        for g, (ws, per_row) in enumerate(zip(starts, slabs)):
            qb = q_ref[bb, g * nq:(g + 1) * nq, :]
            kb = k_ref[bb, ws * GRID_W:ws * GRID_W + nk, :]
            vb = v_ref[bb, ws * GRID_W:ws * GRID_W + nk, :]
            s = lax.dot_general(qb, kb, _NT, preferred_element_type=jnp.float32)
            t_rows = []
            for rr, row_pairs in enumerate(per_row):
                pieces = []
                for p, slab in enumerate(row_pairs):
                    width = min(pw, nk - p * pw)
                    pieces.append(s[rr * GRID_W:(rr + 1) * GRID_W, p * pw:p * pw + width]
                                  + bank_ref[0, slab, :, :width])
                t_rows.append(jnp.concatenate(pieces, axis=1))
            t = jnp.concatenate(t_rows, axis=0)
            m = jnp.max(t, axis=-1, keepdims=True)
            e = jnp.exp2(t - m).astype(jnp.bfloat16)
            ol = jnp.dot(e, _with_ones(vb), preferred_element_type=jnp.float32)
            o_ref[bb, g * nq:(g + 1) * nq, :] = (ol[:, :HEAD_DIM] / ol[:, HEAD_DIM:]).astype(o_ref.dtype)
        return carry

    lax.fori_loop(0, q_ref.shape[0], one, 0)


def _attn_c(p3, tbl, starts, slabs):
    b, s, _ = p3.shape
    nb = math.gcd(b, ATTN_BATCH)
    return pl.pallas_call(
        functools.partial(_attn_c_kernel, starts=starts, slabs=slabs),
        grid=(C_HEADS, b // nb),
        in_specs=[
            pl.BlockSpec((nb, s, HEAD_DIM), lambda h, bi: (bi, 0, h)),
            pl.BlockSpec((nb, s, HEAD_DIM), lambda h, bi: (bi, 0, C_HEADS + h)),
            pl.BlockSpec((nb, s, HEAD_DIM), lambda h, bi: (bi, 0, 2 * C_HEADS + h)),
            pl.BlockSpec((1,) + tbl.shape[1:], lambda h, bi: (h, 0, 0, 0)),
        ],
        out_specs=pl.BlockSpec((nb, s, HEAD_DIM), lambda h, bi: (bi, 0, h)),
        out_shape=jax.ShapeDtypeStruct((b, s, C_HEADS * HEAD_DIM), jnp.bfloat16),
        compiler_params=_cparams(("parallel", "parallel")),
        name="attn_c",
    )(p3, p3, p3, tbl)


def _gate_out_kernel(*refs, n_y, final_norm):
    y_refs = refs[:n_y]
    z_ref, w_ref, x_ref = refs[n_y:n_y + 3]
    o_ref = refs[-1]
    y = jnp.concatenate([r[...] for r in y_refs], axis=1)
    z = z_ref[...]
    gated = y * (z / (1.0 + jnp.exp(-z)))
    out = x_ref[...] + jnp.dot(gated, w_ref[...], preferred_element_type=jnp.float32)
    if final_norm:
        g_ref = refs[-2]
        ms = jnp.mean(out * out, axis=-1, keepdims=True)
        out = out * lax.rsqrt(ms + RMS_EPS) * g_ref[...]
    o_ref[...] = out


def _gate_out(ys, z2d, w, x2d, g_final, *, tm, name):
    m, d = x2d.shape
    in_specs = [pl.BlockSpec((tm, y.shape[1]), lambda i: (i, 0)) for y in ys]
    in_specs += [pl.BlockSpec((tm, d), lambda i: (i, 0)), pl.BlockSpec(w.shape, lambda i: (0, 0)),
                 pl.BlockSpec((tm, d), lambda i: (i, 0))]
    args = list(ys) + [z2d, w, x2d]
    if g_final is not None:
        in_specs.append(pl.BlockSpec((1, d), lambda i: (0, 0)))
        args.append(g_final.reshape(1, d))
    return pl.pallas_call(
        functools.partial(_gate_out_kernel, n_y=len(ys), final_norm=g_final is not None),
        grid=(m // tm,),
        in_specs=in_specs,
        out_specs=pl.BlockSpec((tm, d), lambda i: (i, 0)),
        out_shape=jax.ShapeDtypeStruct((m, d), jnp.float32),
        compiler_params=_cparams(("parallel",)),
        name=name,
    )(*args)


def kernel(x, ln_ab, w_in_ab, sink_a, w_out_ab, ln_c, w_in_c, rpb_c, w_out_c, ln_f):
    b, s, d = x.shape
    assert s == 2048 and d == 2048, "tiling below is written for SEQ = D_MODEL = 2048"
    assert ln_ab.shape[0] == 1 and ln_c.shape[0] == 1, "depth-2 trunk: one layer of each kind"
    m = b * s
    bf16 = jnp.bfloat16
    x2d = x.reshape(m, d)

    slopes = _alibi_slopes(A_Q_HEADS + B_HEADS)
    aw = A_HALF_WINDOW
    tbl_a = jnp.asarray(np.stack([_band_bias(aw, 3 * aw, off, aw, 1, slopes[:A_Q_HEADS]) for off in (0, aw, 2 * aw)],
                                 axis=1))
    qb, kw = 2 * B_HALF, 4 * B_HALF
    tbl_b = jnp.asarray(np.stack(
        [_band_bias(qb, kw, off, B_HALF, dil, slopes[A_Q_HEADS:]) for dil in (1, 4) for off in (0, B_HALF, 2 * B_HALF)]
        + [_band_bias(qb, kw, 0, B_HALF, B_CONFIGS[2][1], slopes[A_Q_HEADS:])], axis=1))
    na_starts, na_slabs, na_pairs = _na_plan(s // GRID_W)
    tbl_c = _na_bias_bank(rpb_c[0], na_pairs)

    hd = HEAD_DIM
    q_cols_ab = ((A_Q_COL * hd, A_K_COL * hd), (B_Q_COL * hd, B_K_COL * hd))
    p, z_ab, w_out_ab16, w_in_c16, w_out_c16 = _norm_proj(
        x2d, ln_ab[0], w_in_ab[0].astype(bf16), q_cols_ab, tm=512, name="norm_proj_ab",
        cast_along=(w_out_ab[0], w_in_c[0], w_out_c[0]))
    p3 = p.reshape(b, s, p.shape[1])
    ya = _attn_a(p3, sink_a[0], tbl_a)
    yb = _attn_b(p3, tbl_b)
    x1 = _gate_out([ya.reshape(m, -1), yb.reshape(m, -1)], z_ab, w_out_ab16, x2d, None, tm=512, name="gate_out_ab")

    pc, z_c = _norm_proj(x1, ln_c[0], w_in_c16, ((0, C_HEADS * hd),), tm=512, name="norm_proj_c")
    yc = _attn_c(pc.reshape(b, s, pc.shape[1]), tbl_c, na_starts, na_slabs)
    out = _gate_out([yc.reshape(m, -1)], z_c, w_out_c16, x1, ln_f, tm=512, name="gate_out_c")
    return out.reshape(b, s, d)
```

```python
import functools
import math

import numpy as np
import jax
import jax.numpy as jnp
from jax import lax
from jax.experimental import pallas as pl
from jax.experimental.pallas import tpu as pltpu

HEAD_DIM = 128
A_Q_HEADS = 8
A_KV_HEADS = 2
A_GROUP = A_Q_HEADS // A_KV_HEADS
A_HALF_WINDOW = 128
B_HEADS = 8
B_CONFIGS = ((128, 1), (512, 4), (2048, 16))
B_HALF = 64
C_HEADS = 16
GRID_W = 64
NA_KH = 8
NA_KW = 16
RMS_EPS = 1e-5
NEG = float("-inf")
LOG2E = 1.4426950408889634
QK_SCALE = HEAD_DIM ** -0.5 * LOG2E

A_Q_COL = 0
A_K_COL = A_Q_HEADS
A_V_COL = A_K_COL + A_KV_HEADS
B_Q_COL = A_V_COL + A_KV_HEADS
B_K_COL = B_Q_COL + B_HEADS
B_V_COL = B_K_COL + B_HEADS

NA_QROWS = 2
NA_KROWS = NA_QROWS + NA_KH - 1

ATTN_BATCH = 4
A_BATCH = 2

VMEM_LIMIT = 60 * 1024 * 1024

_NT = (((1,), (1,)), ((), ()))


def _cparams(sem):
    return pltpu.CompilerParams(dimension_semantics=sem, vmem_limit_bytes=VMEM_LIMIT)


def _edge_variant(i, n):
    return 0 if i == 0 else (2 if i == n - 1 else 1)


def _gated(y, z):
    return y.astype(jnp.bfloat16) * (z / (1.0 + jnp.exp(-z)))


def _with_ones(vb):
    return jnp.concatenate([vb, jnp.ones_like(vb)], axis=1)


def _norm_proj_kernel(*refs, n_cast):
    x_ref, g_ref, w_ref, cs_ref = refs[:4]
    cast_in = refs[4:4 + n_cast]
    qkv_ref, z_ref = refs[4 + n_cast:6 + n_cast]
    cast_out = refs[6 + n_cast:]
    x = x_ref[...]
    ms = jnp.mean(x * x, axis=-1, keepdims=True)
    hn = (x * lax.rsqrt(ms + RMS_EPS) * g_ref[...]).astype(jnp.bfloat16)
    acc = jnp.dot(hn, w_ref[...], preferred_element_type=jnp.float32)
    res = (acc * cs_ref[...]).astype(qkv_ref.dtype)
    n_qkv = qkv_ref.shape[1]
    qkv_ref[...] = res[:, :n_qkv]
    z_ref[...] = res[:, n_qkv:]
    for src, dst in zip(cast_in, cast_out):
        dst[...] = src[...].astype(dst.dtype)


def _norm_proj(x2d, g, w, q_cols, *, tm, name, cast_along=()):
    m, d = x2d.shape
    n = w.shape[1]
    steps = m // tm
    cs = np.ones((1, n), np.float32)
    for lo, hi in q_cols:
        cs[:, lo:hi] = QK_SCALE
    cast_specs = [pl.BlockSpec((c.shape[0] // steps, c.shape[1]), lambda i: (i, 0)) for c in cast_along]
    return pl.pallas_call(
        functools.partial(_norm_proj_kernel, n_cast=len(cast_along)),
        grid=(steps,),
        in_specs=[
            pl.BlockSpec((tm, d), lambda i: (i, 0)),
            pl.BlockSpec((1, d), lambda i: (0, 0)),
            pl.BlockSpec((d, n), lambda i: (0, 0), pipeline_mode=pl.Buffered(1)),
            pl.BlockSpec((1, n), lambda i: (0, 0)),
        ] + cast_specs,
        out_specs=[pl.BlockSpec((tm, n - d), lambda i: (i, 0)), pl.BlockSpec((tm, d), lambda i: (i, 0))] + cast_specs,
        out_shape=[jax.ShapeDtypeStruct((m, n - d), jnp.bfloat16), jax.ShapeDtypeStruct((m, d), jnp.bfloat16)]
        + [jax.ShapeDtypeStruct(c.shape, jnp.bfloat16) for c in cast_along],
        compiler_params=_cparams(("parallel",)),
        name=name,
    )(x2d, g.reshape(1, d), w, jnp.asarray(cs), *cast_along)


def _alibi_slopes(n):
    return np.exp2(np.float32(-8.0) * np.arange(1, n + 1, dtype=np.float32) / np.float32(n))


def _band_bias(nq, nk, off, half, dist_scale, slopes):
    rel = np.arange(nk)[None, :] - np.arange(nq)[:, None] - off
    valid = np.abs(rel) <= half
    dist = (np.abs(rel) * dist_scale).astype(np.float32)
    bias = -slopes[:, None, None] * dist
    return np.where(valid[None], bias * np.float32(LOG2E), np.float32(NEG))


def _na_plan(rows):
    masked = 2 * NA_KH - 1
    starts, slabs, pairs = [], [], []
    for g in range(rows // NA_QROWS):
        ws = int(np.clip(g * NA_QROWS - NA_KH // 2, 0, rows - NA_KROWS))
        per_row = []
        for rr in range(NA_QROWS):
            r = g * NA_QROWS + rr
            r0 = int(np.clip(r - NA_KH // 2, 0, rows - NA_KH))
            assert ws <= r0 and r0 + NA_KH <= ws + NA_KROWS
            ids = [kr - r + NA_KH - 1 if r0 <= kr < r0 + NA_KH else masked for kr in range(ws, ws + NA_KROWS)]
            ids += [masked] * (len(ids) % 2)
            row_pairs = []
            for p in range(0, len(ids), 2):
                pair = (ids[p], ids[p + 1])
                if pair not in pairs:
                    pairs.append(pair)
                row_pairs.append(pairs.index(pair))
            per_row.append(tuple(row_pairs))
        starts.append(ws)
        slabs.append(tuple(per_row))
    return tuple(starts), tuple(slabs), tuple(pairs)


def _na_bias_bank(rpb, pairs):
    h, n_dr, n_dc = rpb.shape
    cq = np.arange(GRID_W)
    c0 = np.clip(cq - NA_KW // 2, 0, GRID_W - NA_KW)
    ck = np.arange(GRID_W)
    col_valid = (ck[None, :] >= c0[:, None]) & (ck[None, :] < c0[:, None] + NA_KW)
    dc = ck[None, :] - cq[:, None] + NA_KW - 1
    onehot = (dc[None] == np.arange(n_dc)[:, None, None]) & col_valid[None]
    rhs = np.zeros((2, n_dc, GRID_W, 2, GRID_W), np.float32)
    for j in range(2):
        rhs[j, :, :, j, :] = onehot
    sel = np.asarray(pairs, np.int32)
    rp = jnp.concatenate([rpb * LOG2E, jnp.zeros((h, 1, n_dc), jnp.float32)], axis=1)
    lhs = jnp.take(rp, jnp.asarray(sel.reshape(-1)), axis=1).reshape(h, len(pairs), 2 * n_dc)
    bank = jnp.einsum("hnx,xqk->hnqk", lhs, jnp.asarray(rhs.reshape(2 * n_dc, GRID_W, 2 * GRID_W)),
                      precision=lax.Precision.HIGHEST)
    valid = (sel != n_dr)[:, None, :, None] & col_valid[None, :, None, :]
    return jnp.where(jnp.asarray(valid.reshape(len(pairs), GRID_W, 2 * GRID_W))[None], bank, NEG)


def _attn_a_kernel(sink_ref, q_ref, k_ref, v_ref, tbl_ref, z_ref, o_ref, *, seq):
    blk = A_HALF_WINDOW
    nk = 3 * blk
    nblk = seq // blk
    g = pl.program_id(0)
    sinks = [sink_ref[g * A_GROUP + j] * LOG2E for j in range(A_GROUP)]

    def block(bb, i):
        qrow = i * blk
        krow = min(max((i - 1) * blk, 0), seq - nk)
        var = _edge_variant(i, nblk)
        q4 = q_ref[bb, pl.ds(qrow, blk), :]
        kb = k_ref[bb, pl.ds(krow, nk), :]
        v1 = _with_ones(v_ref[bb, pl.ds(krow, nk), :])
        outs = []
        for j in range(A_GROUP):
            s = lax.dot_general(q4[:, j * HEAD_DIM:(j + 1) * HEAD_DIM], kb, _NT, preferred_element_type=jnp.float32)
            t = s + tbl_ref[j, var]
            m = jnp.maximum(jnp.max(t, axis=-1, keepdims=True), sinks[j])
            e = jnp.exp2(t - m).astype(jnp.bfloat16)
            ol = jnp.dot(e, v1, preferred_element_type=jnp.float32)
            outs.append(ol[:, :HEAD_DIM] / (ol[:, HEAD_DIM:] + jnp.exp2(sinks[j] - m)))
        o_ref[bb, pl.ds(qrow, blk), :] = _gated(jnp.concatenate(outs, axis=1), z_ref[bb, pl.ds(qrow, blk), :])

    def body(bb, carry):
        for i in range(nblk):
            block(bb, i)
        return carry

    lax.fori_loop(0, q_ref.shape[0], body, 0)


def _attn_a(p3, z3, sink, tbl):
    b, s, _ = p3.shape
    gw = A_GROUP * HEAD_DIM
    nb = math.gcd(b, A_BATCH)
    return pl.pallas_call(
        functools.partial(_attn_a_kernel, seq=s),
        grid=(A_KV_HEADS, b // nb),
        in_specs=[
            pl.BlockSpec(memory_space=pltpu.SMEM),
            pl.BlockSpec((nb, s, gw), lambda g, bi: (bi, 0, g)),
            pl.BlockSpec((nb, s, HEAD_DIM), lambda g, bi: (bi, 0, A_K_COL + g)),
            pl.BlockSpec((nb, s, HEAD_DIM), lambda g, bi: (bi, 0, A_V_COL + g)),
            pl.BlockSpec((A_GROUP,) + tbl.shape[1:], lambda g, bi: (g, 0, 0, 0)),
            pl.BlockSpec((nb, s, gw), lambda g, bi: (bi, 0, g)),
        ],
        out_specs=pl.BlockSpec((nb, s, gw), lambda g, bi: (bi, 0, g)),
        out_shape=jax.ShapeDtypeStruct((b, s, A_Q_HEADS * HEAD_DIM), jnp.bfloat16),
        compiler_params=_cparams(("parallel", "parallel")),
        name="attn_a",
    )(sink, p3, p3, p3, tbl, z3)


def _attn_b_kernel(*refs, seq):
    batch_rows = refs[0].shape[0]
    lax.fori_loop(0, batch_rows, functools.partial(_attn_b_one, refs=refs, seq=seq), 0)


def _attn_b_one(bb, carry, *, refs, seq):
    (q_ref, k_ref, v_ref, tbl_ref, z_ref, o_ref, qf, kf, vf, q4, k4, v4, of,
     acc0, acc1, acc2, m0, m1, m2, l0, l1, l2) = refs
    qb_rows = 2 * B_HALF
    kw = 4 * B_HALF
    fold = B_CONFIGS[1][1]

    def attend(qb, kb, vb, tb):
        s = lax.dot_general(qb, kb, _NT, preferred_element_type=jnp.float32)
        t = s + tb
        m = jnp.max(t, axis=-1, keepdims=True)
        e = jnp.exp2(t - m).astype(jnp.bfloat16)
        al = jnp.dot(e, _with_ones(vb), preferred_element_type=jnp.float32)
        return al[:, :HEAD_DIM], jnp.broadcast_to(m, (qb.shape[0], HEAD_DIM)), al[:, HEAD_DIM:]

    nblk = seq // qb_rows

    for i in range(nblk):
        qrow = i * qb_rows
        krow = min(max(i * qb_rows - B_HALF, 0), seq - kw)
        a, m, l = attend(q_ref[bb, pl.ds(qrow, qb_rows), :], k_ref[bb, pl.ds(krow, kw), :],
                         v_ref[bb, pl.ds(krow, kw), :], tbl_ref[0, _edge_variant(i, nblk)])
        acc0[pl.ds(qrow, qb_rows), :] = a
        m0[pl.ds(qrow, qb_rows), :] = m
        l0[pl.ds(qrow, qb_rows), :] = l

    qf[...] = q_ref[bb].astype(jnp.float32)
    kf[...] = k_ref[bb].astype(jnp.float32)
    vf[...] = v_ref[bb].astype(jnp.float32)

    def dilated(d, var0, acc, mm, ll):
        ln = seq // d
        nb = ln // qb_rows

        for r in range(d):
            if d == fold:
                cls = [src[pl.ds(r, ln, stride=d), :] for src in (qf, kf, vf)]
                for dst, val in zip((q4, k4, v4), cls):
                    dst[pl.ds(r * ln, ln), :] = val
            else:
                first = (r % fold) * (seq // fold) + r // fold
                cls = [src[pl.ds(first, ln, stride=d // fold), :] for src in (q4, k4, v4)]
            qr, kr, vr = (val.astype(jnp.bfloat16) for val in cls)
            for i in range(nb):
                if nb == 1:
                    k0, width, var = 0, ln, var0
                else:
                    k0 = min(max(i * qb_rows - B_HALF, 0), ln - kw)
                    width = kw
                    var = var0 + (0 if i == 0 else (2 if i == nb - 1 else 1))
                a, m, l = attend(qr[i * qb_rows:(i + 1) * qb_rows], kr[k0:k0 + width], vr[k0:k0 + width],
                                 tbl_ref[0, var, :, :width])
                start = (r % fold) * (seq // fold) + r // fold + (d // fold) * qb_rows * i
                rows = pl.ds(start, qb_rows) if d == fold else pl.ds(start, qb_rows, stride=d // fold)
                acc[rows, :] = a
                mm[rows, :] = m
                ll[rows, :] = l

    dilated(B_CONFIGS[1][1], 3, acc1, m1, l1)
    dilated(B_CONFIGS[2][1], 6, acc2, m2, l2)

    merge_rows = 2 * qb_rows
    chunks_per_class = seq // fold // merge_rows

    def merge(c, carry):
        rows = pl.ds(pl.multiple_of(c * merge_rows, merge_rows), merge_rows)
        nat = pl.ds(c // chunks_per_class + fold * merge_rows * (c % chunks_per_class), merge_rows, stride=fold)
        ma, mb, mc = m0[nat, :], m1[rows, :], m2[rows, :]
        mx = jnp.maximum(jnp.maximum(ma, mb), mc)
        wa, wb, wc = jnp.exp2(ma - mx), jnp.exp2(mb - mx), jnp.exp2(mc - mx)
        num = wa * acc0[nat, :] + wb * acc1[rows, :] + wc * acc2[rows, :]
        den = wa * l0[nat, :] + wb * l1[rows, :] + wc * l2[rows, :]
        of[nat, :] = num / den
        return carry

    lax.fori_loop(0, seq // merge_rows, merge, 0)
    o_ref[bb] = _gated(of[...], z_ref[bb])
    return carry


def _attn_b(p3, z3, tbl):
    b, s, _ = p3.shape
    f32_slab = pltpu.VMEM((s, HEAD_DIM), jnp.float32)
    nb = math.gcd(b, ATTN_BATCH)
    return pl.pallas_call(
        functools.partial(_attn_b_kernel, seq=s),
        grid=(B_HEADS, b // nb),
        in_specs=[
            pl.BlockSpec((nb, s, HEAD_DIM), lambda h, bi: (bi, 0, B_Q_COL + h)),
            pl.BlockSpec((nb, s, HEAD_DIM), lambda h, bi: (bi, 0, B_K_COL + h)),
            pl.BlockSpec((nb, s, HEAD_DIM), lambda h, bi: (bi, 0, B_V_COL + h)),
            pl.BlockSpec((1,) + tbl.shape[1:], lambda h, bi: (h, 0, 0, 0)),
            pl.BlockSpec((nb, s, HEAD_DIM), lambda h, bi: (bi, 0, A_Q_HEADS + h)),
        ],
        out_specs=pl.BlockSpec((nb, s, HEAD_DIM), lambda h, bi: (bi, 0, h)),
        out_shape=jax.ShapeDtypeStruct((b, s, B_HEADS * HEAD_DIM), jnp.bfloat16),
        scratch_shapes=[f32_slab] * 16,
        compiler_params=_cparams(("parallel", "parallel")),
        name="attn_b",
    )(p3, p3, p3, tbl, z3)


def _attn_c_kernel(q_ref, k_ref, v_ref, bank_ref, z_ref, o_ref, *, starts, slabs):
    nq = NA_QROWS * GRID_W
    nk = NA_KROWS * GRID_W
    pw = 2 * GRID_W

    def one(bb, carry):
        for g, (ws, per_row) in enumerate(zip(starts, slabs)):
            qb = q_ref[bb, g * nq:(g + 1) * nq, :]
            kb = k_ref[bb, ws * GRID_W:ws * GRID_W + nk, :]
            vb = v_ref[bb, ws * GRID_W:ws * GRID_W + nk, :]
            s = lax.dot_general(qb, kb, _NT, preferred_element_type=jnp.float32)
            t_rows = []
            for rr, row_pairs in enumerate(per_row):
                pieces = []
                for p, slab in enumerate(row_pairs):
                    width = min(pw, nk - p * pw)
                    pieces.append(s[rr * GRID_W:(rr + 1) * GRID_W, p * pw:p * pw + width]
                                  + bank_ref[0, slab, :, :width])
                t_rows.append(jnp.concatenate(pieces, axis=1))
            t = jnp.concatenate(t_rows, axis=0)
            m = jnp.max(t, axis=-1, keepdims=True)
            e = jnp.exp2(t - m).astype(jnp.bfloat16)
            ol = jnp.dot(e, _with_ones(vb), preferred_element_type=jnp.float32)
            o_ref[bb, g * nq:(g + 1) * nq, :] = _gated(ol[:, :HEAD_DIM] / ol[:, HEAD_DIM:],
                                                       z_ref[bb, g * nq:(g + 1) * nq, :])
        return carry

    lax.fori_loop(0, q_ref.shape[0], one, 0)


def _attn_c(p3, z3, tbl, starts, slabs):
    b, s, _ = p3.shape
    nb = math.gcd(b, ATTN_BATCH)
    return pl.pallas_call(
        functools.partial(_attn_c_kernel, starts=starts, slabs=slabs),
        grid=(C_HEADS, b // nb),
        in_specs=[
            pl.BlockSpec((nb, s, HEAD_DIM), lambda h, bi: (bi, 0, h)),
            pl.BlockSpec((nb, s, HEAD_DIM), lambda h, bi: (bi, 0, C_HEADS + h)),
            pl.BlockSpec((nb, s, HEAD_DIM), lambda h, bi: (bi, 0, 2 * C_HEADS + h)),
            pl.BlockSpec((1,) + tbl.shape[1:], lambda h, bi: (h, 0, 0, 0)),
            pl.BlockSpec((nb, s, HEAD_DIM), lambda h, bi: (bi, 0, h)),
        ],
        out_specs=pl.BlockSpec((nb, s, HEAD_DIM), lambda h, bi: (bi, 0, h)),
        out_shape=jax.ShapeDtypeStruct((b, s, C_HEADS * HEAD_DIM), jnp.bfloat16),
        compiler_params=_cparams(("parallel", "parallel")),
        name="attn_c",
    )(p3, p3, p3, tbl, z3)


def _gate_out_kernel(*refs, n_y, final_norm):
    y_refs = refs[:n_y]
    w_ref, x_ref = refs[n_y:n_y + 2]
    o_ref = refs[-1]
    gated = jnp.concatenate([r[...] for r in y_refs], axis=1)
    out = x_ref[...] + jnp.dot(gated, w_ref[...], preferred_element_type=jnp.float32)
    if final_norm:
        g_ref = refs[-2]
        ms = jnp.mean(out * out, axis=-1, keepdims=True)
        out = out * lax.rsqrt(ms + RMS_EPS) * g_ref[...]
    o_ref[...] = out


def _gate_out(ys, w, x2d, g_final, *, tm, name):
    m, d = x2d.shape
    in_specs = [pl.BlockSpec((tm, y.shape[1]), lambda i: (i, 0)) for y in ys]
    in_specs += [pl.BlockSpec(w.shape, lambda i: (0, 0)), pl.BlockSpec((tm, d), lambda i: (i, 0))]
    args = list(ys) + [w, x2d]
    if g_final is not None:
        in_specs.append(pl.BlockSpec((1, d), lambda i: (0, 0)))
        args.append(g_final.reshape(1, d))
    return pl.pallas_call(
        functools.partial(_gate_out_kernel, n_y=len(ys), final_norm=g_final is not None),
        grid=(m // tm,),
        in_specs=in_specs,
        out_specs=pl.BlockSpec((tm, d), lambda i: (i, 0)),
        out_shape=jax.ShapeDtypeStruct((m, d), jnp.float32),
        compiler_params=_cparams(("parallel",)),
        name=name,
    )(*args)


def kernel(x, ln_ab, w_in_ab, sink_a, w_out_ab, ln_c, w_in_c, rpb_c, w_out_c, ln_f):
    b, s, d = x.shape
    assert s == 2048 and d == 2048, "tiling below is written for SEQ = D_MODEL = 2048"
    assert ln_ab.shape[0] == 1 and ln_c.shape[0] == 1, "depth-2 trunk: one layer of each kind"
    m = b * s
    bf16 = jnp.bfloat16
    x2d = x.reshape(m, d)

    slopes = _alibi_slopes(A_Q_HEADS + B_HEADS)
    aw = A_HALF_WINDOW
    tbl_a = jnp.asarray(np.stack([_band_bias(aw, 3 * aw, off, aw, 1, slopes[:A_Q_HEADS]) for off in (0, aw, 2 * aw)],
                                 axis=1))
    qb, kw = 2 * B_HALF, 4 * B_HALF
    tbl_b = jnp.asarray(np.stack(
        [_band_bias(qb, kw, off, B_HALF, dil, slopes[A_Q_HEADS:]) for dil in (1, 4) for off in (0, B_HALF, 2 * B_HALF)]
        + [_band_bias(qb, kw, 0, B_HALF, B_CONFIGS[2][1], slopes[A_Q_HEADS:])], axis=1))
    na_starts, na_slabs, na_pairs = _na_plan(s // GRID_W)
    tbl_c = _na_bias_bank(rpb_c[0], na_pairs)

    hd = HEAD_DIM
    q_cols_ab = ((A_Q_COL * hd, A_K_COL * hd), (B_Q_COL * hd, B_K_COL * hd))
    p, z_ab, w_out_ab16, w_in_c16, w_out_c16 = _norm_proj(
        x2d, ln_ab[0], w_in_ab[0].astype(bf16), q_cols_ab, tm=512, name="norm_proj_ab",
        cast_along=(w_out_ab[0], w_in_c[0], w_out_c[0]))
    p3 = p.reshape(b, s, p.shape[1])
    z3 = z_ab.reshape(b, s, d)
    ya = _attn_a(p3, z3, sink_a[0], tbl_a)
    yb = _attn_b(p3, z3, tbl_b)
    x1 = _gate_out([ya.reshape(m, -1), yb.reshape(m, -1)], w_out_ab16, x2d, None, tm=512, name="gate_out_ab")

    pc, z_c = _norm_proj(x1, ln_c[0], w_in_c16, ((0, C_HEADS * hd),), tm=512, name="norm_proj_c")
    yc = _attn_c(pc.reshape(b, s, pc.shape[1]), z_c.reshape(b, s, d), tbl_c, na_starts, na_slabs)
    out = _gate_out([yc.reshape(m, -1)], w_out_c16, x1, ln_f, tm=512, name="gate_out_c")
    return out.reshape(b, s, d)
```
